```python
import jax, jax.numpy as jnp
from jax import lax
import numpy as np

D_MODEL = 1024
BATCH = 4
SEQ = 4096
DEPTH = 1
DEC_BATCH = 128
DEC_SEQ = 4
PAST_LEN = 8192
PAGE_SIZE = 128

D_MIX = D_MODEL
D_RNN = D_MIX // 2
D_ATTN = D_MIX - D_RNN
N_RNN_BLOCKS = 8
RNN_BLOCK = D_RNN // N_RNN_BLOCKS
CONV_WIDTH = 4
LRU_C = 8.0
N_HEADS = 8
HEAD_DIM = D_ATTN // N_HEADS
DILATED_PATTERNS = ((128, 1), (512, 4), (2048, 16))
CLASS_BLOCK = 128
MAX_WINDOW = max(w for w, _ in DILATED_PATTERNS)
D_IN = 2 * D_RNN + 3 * D_ATTN
N_KEYS = 128
N_EXPERTS = N_KEYS * N_KEYS
PEER_HEADS = 8
PEER_QDIM = 256
PEER_HALF = PEER_QDIM // 2
PEER_TOPK = 16
PEER_TOKEN_BLOCK = 256
EPS = 1e-6

kernel_name = 'hymba_rglru_dilated_peer_step'


def rmsnorm(x, g):
    xf = x.astype(jnp.float32)
    y = xf * lax.rsqrt(jnp.mean(xf * xf, axis=-1, keepdims=True) + EPS)
    return (y * g.astype(jnp.float32)).astype(x.dtype)


def causal_conv(xp, w, b):
    y = lax.conv_general_dilated(xp, w[:, None, :], window_strides=(1,), padding='VALID',
                                 dimension_numbers=('NWC', 'WIO', 'NWC'),
                                 feature_group_count=xp.shape[-1])
    return y + b


def rg_lru(x, w_a, b_a, w_x, b_x, lam, h0):
    B, T, _ = x.shape
    xb = x.reshape(B, T, N_RNN_BLOCKS, RNN_BLOCK)
    r = jax.nn.sigmoid((jnp.einsum('btnc,ncd->btnd', xb, w_a).reshape(B, T, D_RNN) + b_a).astype(jnp.float32))
    i = jax.nn.sigmoid((jnp.einsum('btnc,ncd->btnd', xb, w_x).reshape(B, T, D_RNN) + b_x).astype(jnp.float32))
    log_a = -LRU_C * r * jax.nn.softplus(-lam.astype(jnp.float32))
    a = jnp.exp(log_a)
    u = jnp.sqrt(-jnp.expm1(2.0 * log_a)) * (i * x.astype(jnp.float32))

    def step(h, au):
        h = au[0] * h + au[1]
        return h, h

    h_last, hs = lax.scan(step, h0.astype(jnp.float32), (jnp.swapaxes(a, 0, 1), jnp.swapaxes(u, 0, 1)))
    return jnp.swapaxes(hs, 0, 1).astype(x.dtype), h_last.astype(x.dtype)


def to_classes(t, dilation, s_pad):
    B, S, H, E = t.shape
    t = jnp.pad(t, ((0, 0), (0, s_pad - S), (0, 0), (0, 0)))
    t = jnp.swapaxes(t.reshape(B, s_pad // dilation, dilation, H, E), 1, 2)
    return t.reshape(B, dilation, s_pad // (dilation * CLASS_BLOCK), CLASS_BLOCK, H, E)


def from_classes(t, S):
    B, r, nb, qb = t.shape[:4]
    rest = t.shape[4:]
    t = jnp.swapaxes(t.reshape((B, r, nb * qb) + rest), 1, 2)
    return t.reshape((B, r * nb * qb) + rest)[:, :S]


def dilated_pattern_prompt(q, k, v, window, dilation):
    B, S, H, E = q.shape
    span = dilation * CLASS_BLOCK
    s_pad = -(-S // span) * span
    qc = to_classes(q, dilation, s_pad)
    kc = to_classes(k, dilation, s_pad)
    vc = to_classes(v, dilation, s_pad)
    nb = qc.shape[2]

    def with_prev(t):
        prev = jnp.pad(t, ((0, 0), (0, 0), (1, 0), (0, 0), (0, 0), (0, 0)))[:, :, :-1]
        return jnp.concatenate([prev, t], axis=3)

    kk, vv = with_prev(kc), with_prev(vc)
    logits = jnp.einsum('brnqhe,brnkhe->brnhqk', qc, kk) * (HEAD_DIM ** -0.5)
    qi = jnp.arange(CLASS_BLOCK)[:, None]
    ki = jnp.arange(2 * CLASS_BLOCK)[None, :]
    dist = qi + CLASS_BLOCK - ki
    blk = jnp.arange(nb)[:, None, None]
    valid = (dist >= 0) & (dist <= window // dilation) & ((ki >= CLASS_BLOCK) | (blk > 0))
    logits = jnp.where(valid[:, None], logits, -jnp.inf)
    m = jnp.max(logits, axis=-1)
    p = jnp.exp(logits - m[..., None])
    den = jnp.sum(p, axis=-1)
    num = jnp.einsum('brnhqk,brnkhe->brnqhe', p, vv)
    return (from_classes(jnp.swapaxes(m, 3, 4), S), from_classes(jnp.swapaxes(den, 3, 4), S),
            from_classes(num, S))


def dilated_pattern_sample(q, k_all, v_all, window, dilation, n_past):
    T = q.shape[1]
    idx = n_past + jnp.arange(T)[:, None] - dilation * jnp.arange(window // dilation + 1)[None, :]
    valid = idx >= 0
    idx = jnp.maximum(idx, 0)
    kg = jnp.take(k_all, idx, axis=1)
    vg = jnp.take(v_all, idx, axis=1)
    logits = jnp.einsum('bjhe,bjkhe->bjhk', q, kg) * (HEAD_DIM ** -0.5)
    logits = jnp.where(valid[None, :, None, :], logits, -jnp.inf)
    m = jnp.max(logits, axis=-1)
    p = jnp.exp(logits - m[..., None])
    den = jnp.sum(p, axis=-1)
    num = jnp.einsum('bjhk,bjkhe->bjhe', p, vg)
    return m, den, num


def mix_patterns(stats):
    m_all = stats[0][0]
    for s in stats[1:]:
        m_all = jnp.maximum(m_all, s[0])
    c0 = jnp.exp(stats[0][0] - m_all)
    den = c0 * stats[0][1]
    num = c0[..., None] * stats[0][2]
    for m, d, n in stats[1:]:
        c = jnp.exp(m - m_all)
        den = den + c * d
        num = num + c[..., None] * n
    return num / den[..., None]


def peer_block(xc, w_query, sub_keys, expert_u, expert_v):
    C = xc.shape[0]
    qh = (xc @ w_query).reshape(C, PEER_HEADS, 2, PEER_HALF)
    scores = jnp.einsum('chpe,hpne->chpn', qh, sub_keys).astype(jnp.float32)
    s, ix = lax.top_k(scores, PEER_TOPK)
    cand = (s[:, :, 0, :, None] + s[:, :, 1, None, :]).reshape(C, PEER_HEADS, PEER_TOPK * PEER_TOPK)
    cid = (ix[:, :, 0, :, None] * N_KEYS + ix[:, :, 1, None, :]).reshape(C, PEER_HEADS, PEER_TOPK * PEER_TOPK)
    top_s, pos = lax.top_k(cand, PEER_TOPK)
    eid = jnp.take_along_axis(cid, pos, axis=-1).reshape(C, PEER_HEADS * PEER_TOPK)
    g = jax.nn.softmax(top_s, axis=-1).reshape(C, PEER_HEADS * PEER_TOPK)
    u = expert_u[eid]
    act = jax.nn.gelu(jnp.einsum('cd,ced->ce', xc, u).astype(jnp.float32))
    return jnp.einsum('ce,ced->cd', (g * act).astype(xc.dtype), expert_v[eid])


def peer(x, w_query, sub_keys, expert_u, expert_v):
    B, T, D = x.shape
    flat = x.reshape(B * T, D)
    n = B * T
    n_pad = -(-n // PEER_TOKEN_BLOCK) * PEER_TOKEN_BLOCK
    flat = jnp.pad(flat, ((0, n_pad - n), (0, 0))).reshape(n_pad // PEER_TOKEN_BLOCK, PEER_TOKEN_BLOCK, D)
    out = lax.map(lambda xc: peer_block(xc, w_query, sub_keys, expert_u, expert_v), flat)
    return out.reshape(n_pad, D)[:n].reshape(B, T, D)


def layer(x, conv_prev, h_prev, k_past, v_past, g_mix, w_in, conv_w, conv_b, w_a, b_a, w_x, b_x,
          lru_lambda, g_rnn_out, g_attn_out, w_out, g_ffn, w_query, sub_keys, expert_u, expert_v):
    B, T, _ = x.shape
    xn = rmsnorm(x, g_mix)
    z = xn @ w_in
    xr, gate, q, k, v = jnp.split(z, [D_RNN, 2 * D_RNN, 2 * D_RNN + D_ATTN, 2 * D_RNN + 2 * D_ATTN], axis=-1)
    conv_in = jnp.concatenate([conv_prev, xr], axis=1)
    xc = causal_conv(conv_in, conv_w, conv_b)
    hs, h_last = rg_lru(xc, w_a, b_a, w_x, b_x, lru_lambda, h_prev)
    y_rnn = hs * jax.nn.gelu(gate)
    k_rows = k.reshape(B, T, N_HEADS, HEAD_DIM)
    v_rows = v.reshape(B, T, N_HEADS, HEAD_DIM)
    qf = q.reshape(B, T, N_HEADS, HEAD_DIM).astype(jnp.float32)
    kf = k_rows.astype(jnp.float32)
    vf = v_rows.astype(jnp.float32)
    if k_past is None:
        stats = [dilated_pattern_prompt(qf, kf, vf, w, d) for w, d in DILATED_PATTERNS]
        keep = min(MAX_WINDOW, T)
        k_rows, v_rows = k_rows[:, T - keep:], v_rows[:, T - keep:]
    else:
        k_all = jnp.concatenate([k_past.astype(jnp.float32), kf], axis=1)
        v_all = jnp.concatenate([v_past.astype(jnp.float32), vf], axis=1)
        stats = [dilated_pattern_sample(qf, k_all, v_all, w, d, k_past.shape[1]) for w, d in DILATED_PATTERNS]
    y_attn = mix_patterns(stats).reshape(B, T, D_ATTN).astype(x.dtype)
    merged = jnp.concatenate([rmsnorm(y_rnn, g_rnn_out), rmsnorm(y_attn, g_attn_out)], axis=-1) @ w_out
    x = x + merged
    x = x + peer(rmsnorm(x, g_ffn), w_query, sub_keys, expert_u, expert_v)
    return x, k_rows, v_rows, conv_in[:, -(CONV_WIDTH - 1):], h_last


def setup_inputs(seed: int = 0) -> dict:
    key = jax.random.key(seed)
    ks = jax.random.split(key, 24)
    f32 = jnp.float32

    def nrm(k, shape, scale):
        return jax.random.normal(k, shape, f32) * scale

    win_buf = min(MAX_WINDOW, PAST_LEN)
    a0 = jax.random.uniform(ks[10], (DEPTH, D_RNN), f32, 0.9, 0.999)
    return {
        'x_prompt': nrm(ks[0], (BATCH, SEQ, D_MODEL), 1.0),
        'x_sample': nrm(ks[1], (DEC_BATCH, DEC_SEQ, D_MODEL), 1.0),
        'cache_k': nrm(ks[2], (DEPTH, DEC_BATCH, win_buf, N_HEADS, HEAD_DIM), 1.0),
        'cache_v': nrm(ks[3], (DEPTH, DEC_BATCH, win_buf, N_HEADS, HEAD_DIM), 1.0),
        'state_conv': nrm(ks[4], (DEPTH, DEC_BATCH, CONV_WIDTH - 1, D_RNN), 1.0),
        'state_h': nrm(ks[5], (DEPTH, DEC_BATCH, D_RNN), 0.5),
        'g_mix': 1.0 + nrm(ks[6], (DEPTH, D_MODEL), 0.05),
        'w_in': nrm(ks[7], (DEPTH, D_MODEL, D_IN), D_MODEL ** -0.5),
        'conv_w': nrm(ks[8], (DEPTH, CONV_WIDTH, D_RNN), CONV_WIDTH ** -0.5),
        'conv_b': nrm(ks[9], (DEPTH, D_RNN), 0.02),
        'w_a': nrm(ks[11], (DEPTH, N_RNN_BLOCKS, RNN_BLOCK, RNN_BLOCK), RNN_BLOCK ** -0.5),
        'b_a': nrm(ks[12], (DEPTH, D_RNN), 0.02),
        'w_x': nrm(ks[13], (DEPTH, N_RNN_BLOCKS, RNN_BLOCK, RNN_BLOCK), RNN_BLOCK ** -0.5),
        'b_x': nrm(ks[14], (DEPTH, D_RNN), 0.02),
        'lru_lambda': jnp.log(a0) - jnp.log1p(-a0),
        'g_rnn_out': 1.0 + nrm(ks[15], (DEPTH, D_RNN), 0.05),
        'g_attn_out': 1.0 + nrm(ks[16], (DEPTH, D_ATTN), 0.05),
        'w_out': nrm(ks[17], (DEPTH, D_MIX, D_MODEL), D_MIX ** -0.5),
        'g_ffn': 1.0 + nrm(ks[18], (DEPTH, D_MODEL), 0.05),
        'w_query': nrm(ks[19], (DEPTH, D_MODEL, PEER_HEADS * PEER_QDIM), D_MODEL ** -0.5),
        'sub_keys': nrm(ks[20], (DEPTH, PEER_HEADS, 2, N_KEYS, PEER_HALF), PEER_HALF ** -0.5),
        'expert_u': nrm(ks[21], (DEPTH, N_EXPERTS, D_MODEL), D_MODEL ** -0.5),
        'expert_v': nrm(ks[22], (DEPTH, N_EXPERTS, D_MODEL), PEER_HEADS ** -0.5),
        'g_final': 1.0 + nrm(ks[23], (D_MODEL,), 0.05),
    }


def reference(x_prompt, x_sample, cache_k, cache_v, state_conv, state_h, g_mix, w_in, conv_w, conv_b,
              w_a, b_a, w_x, b_x, lru_lambda, g_rnn_out, g_attn_out, w_out, g_ffn, w_query, sub_keys,
              expert_u, expert_v, g_final):
    xp, xs = x_prompt, x_sample
    kp_l, vp_l, ks_l, vs_l, cp_l, cs_l, hp_l, hs_l = [], [], [], [], [], [], [], []
    for l in range(DEPTH):
        w = (g_mix[l], w_in[l], conv_w[l], conv_b[l], w_a[l], b_a[l], w_x[l], b_x[l], lru_lambda[l],
             g_rnn_out[l], g_attn_out[l], w_out[l], g_ffn[l], w_query[l], sub_keys[l], expert_u[l], expert_v[l])
        conv0 = jnp.zeros((xp.shape[0], CONV_WIDTH - 1, D_RNN), xp.dtype)
        h0 = jnp.zeros((xp.shape[0], D_RNN), xp.dtype)
        xp, kp, vp, cp, hp = layer(xp, conv0, h0, None, None, *w)
        xs, ksm, vsm, csm, hsm = layer(xs, state_conv[l], state_h[l], cache_k[l], cache_v[l], *w)
        kp_l.append(kp); vp_l.append(vp); cp_l.append(cp); hp_l.append(hp)
        ks_l.append(ksm); vs_l.append(vsm); cs_l.append(csm); hs_l.append(hsm)
    y_prompt = rmsnorm(xp, g_final)
    y_sample = rmsnorm(xs, g_final)
    return (y_prompt, y_sample, jnp.stack(kp_l), jnp.stack(vp_l), jnp.stack(ks_l), jnp.stack(vs_l),
            jnp.stack(cp_l), jnp.stack(cs_l), jnp.stack(hp_l), jnp.stack(hs_l))
```

```python
import functools

import numpy as np
import jax
import jax.numpy as jnp
from jax import lax
from jax.experimental import pallas as pl
from jax.experimental.pallas import tpu as pltpu

F32 = jnp.float32
BF16 = jnp.bfloat16

EPS = 1e-6
D_RNN = 512
D_ATTN = 512
N_HEADS = 8
HEAD_DIM = 64
CONV_WIDTH = 4
LRU_C = 8.0
DILATED_PATTERNS = ((128, 1), (512, 4), (2048, 16))
MAX_WINDOW = 2048
N_KEYS = 128
PEER_HEADS = 8
PEER_HALF = 128
PEER_TOPK = 16

LANES = 128
SUBLANES = 8
HEAD_PAIR = LANES
NEG = -1e30
VMEM_LIMIT = 48 * 1024 * 1024


def _params(*sem):
    return pltpu.CompilerParams(dimension_semantics=sem, vmem_limit_bytes=VMEM_LIMIT)


def _tile(n, pref):
    return pref if n % pref == 0 else n


def _rms(x, g):
    return x * lax.rsqrt(jnp.mean(x * x, axis=-1, keepdims=True) + EPS) * g


def _gelu(x):
    c = np.float32(np.sqrt(2.0 / np.pi))
    return x * (0.5 * (1.0 + jnp.tanh(c * (x + 0.044715 * (x * x * x)))))


def _sigmoid(x):
    return 1.0 / (1.0 + jnp.exp(-x))


def _multiplicity(delta):
    delta = np.asarray(delta)
    cnt = np.zeros(delta.shape, np.int32)
    for window, dil in DILATED_PATTERNS:
        cnt += ((delta >= 0) & (delta <= window) & (delta % dil == 0)).astype(np.int32)
    return cnt


def _log_multiplicity(delta):
    cnt = _multiplicity(delta)
    return np.where(cnt > 0, np.log(np.maximum(cnt, 1)), NEG).astype(np.float32)


def _inproj_kernel(x_ref, g_ref, w_ref, xr_ref, gate_ref, q_ref, k_ref, v_ref, kf_ref, vf_ref):
    xn = _rms(x_ref[...], g_ref[...])
    z = jnp.dot(xn.astype(BF16), w_ref[...], preferred_element_type=F32)
    xr_ref[...] = z[:, 0:D_RNN]
    gate_ref[...] = z[:, D_RNN:2 * D_RNN]
    o = 2 * D_RNN
    q_ref[...] = (z[:, o:o + D_ATTN] * (HEAD_DIM ** -0.5)).astype(BF16)
    k = z[:, o + D_ATTN:o + 2 * D_ATTN]
    v = z[:, o + 2 * D_ATTN:o + 3 * D_ATTN]
    k_ref[...] = k.astype(BF16)
    v_ref[...] = v.astype(BF16)
    kf_ref[...] = k
    vf_ref[...] = v


def _inproj(x, g_mix, w_in_bf):
    n, d = x.shape
    d_in = w_in_bf.shape[1]
    tm = _tile(n, 512)
    row = lambda i: (i, 0)
    full = lambda i: (0, 0)
    o512 = pl.BlockSpec((tm, 512), row)
    return pl.pallas_call(
        _inproj_kernel,
        grid=(n // tm,),
        in_specs=[pl.BlockSpec((tm, d), row), pl.BlockSpec((1, d), full), pl.BlockSpec((d, d_in), full)],
        out_specs=[o512] * 7,
        out_shape=[jax.ShapeDtypeStruct((n, 512), F32), jax.ShapeDtypeStruct((n, 512), F32),
                   jax.ShapeDtypeStruct((n, 512), BF16), jax.ShapeDtypeStruct((n, 512), BF16),
                   jax.ShapeDtypeStruct((n, 512), BF16), jax.ShapeDtypeStruct((n, 512), F32),
                   jax.ShapeDtypeStruct((n, 512), F32)],
        compiler_params=_params("parallel"),
        name="inproj",
    )(x, g_mix.reshape(1, d), w_in_bf)


def _lru_gates(xc, wa, ba, wx, bx, lam):
    xb = xc.astype(BF16)
    r = _sigmoid(jnp.dot(xb, wa, preferred_element_type=F32) + ba)
    i = _sigmoid(jnp.dot(xb, wx, preferred_element_type=F32) + bx)
    nl = -lam
    softplus = jnp.maximum(nl, 0.0) + jnp.log1p(jnp.exp(-jnp.abs(nl)))
    log_a = (-LRU_C) * r * softplus
    a = jnp.exp(log_a)
    u = jnp.sqrt(-jnp.tanh(log_a) * (a * a + 1.0)) * (i * xc)
    return a, u


def _rglru_prompt_kernel(xr_ref, gate_ref, cw_ref, cb_ref, wa_ref, ba_ref, wx_ref, bx_ref, lam_ref,
                         y_ref, hlast_ref, ext_ref, a_ref, u_ref, hc_ref):
    t = pl.program_id(1)
    tt = xr_ref.shape[1]

    @pl.when(t == 0)
    def _():
        ext_ref[0:SUBLANES, :] = jnp.zeros((SUBLANES, D_RNN), F32)
        hc_ref[...] = jnp.zeros(hc_ref.shape, F32)

    ext_ref[SUBLANES:SUBLANES + tt, :] = xr_ref[0]
    cw = cw_ref[...]
    xc = cb_ref[...]
    for w in range(CONV_WIDTH):
        lo = SUBLANES - (CONV_WIDTH - 1) + w
        xc = xc + cw[w:w + 1, :] * ext_ref[lo:lo + tt, :]
    ext_ref[0:SUBLANES, :] = ext_ref[tt:tt + SUBLANES, :]

    a, u = _lru_gates(xc, wa_ref[...], ba_ref[...], wx_ref[...], bx_ref[...], lam_ref[...])
    a_ref[...] = a
    u_ref[...] = u

    row = lax.broadcasted_iota(jnp.int32, (SUBLANES, D_RNN), 0)

    def group(g, hc):
        off = pl.multiple_of(g * SUBLANES, SUBLANES)
        a8 = a_ref[pl.ds(off, SUBLANES), :]
        u8 = u_ref[pl.ds(off, SUBLANES), :]
        for s in (1, 2, 4):
            a_sh = pltpu.roll(a8, s, 0)
            u_sh = pltpu.roll(u8, s, 0)
            m = row >= s
            u8 = jnp.where(m, u8 + a8 * u_sh, u8)
            a8 = jnp.where(m, a8 * a_sh, a8)
        h8 = u8 + a8 * hc
        u_ref[pl.ds(off, SUBLANES), :] = h8
        return h8[SUBLANES - 1:SUBLANES, :]

    hc = lax.fori_loop(0, tt // SUBLANES, group, hc_ref[0:1, :])
    hc_ref[...] = jnp.broadcast_to(hc, hc_ref.shape)
    hlast_ref[0] = hc
    y_ref[0] = u_ref[...] * _gelu(gate_ref[0])


def _rglru_prompt(xr, gate, cw, cb, wa, ba, wx, bx, lam):
    b, t, c = xr.shape
    tt = _tile(t, 512)
    seq = lambda i, j: (i, j, 0)
    vec = lambda i, j: (0, 0)
    return pl.pallas_call(
        _rglru_prompt_kernel,
        grid=(b, t // tt),
        in_specs=[pl.BlockSpec((1, tt, c), seq), pl.BlockSpec((1, tt, c), seq),
                  pl.BlockSpec((CONV_WIDTH, c), vec), pl.BlockSpec((1, c), vec),
                  pl.BlockSpec((c, c), vec), pl.BlockSpec((1, c), vec),
                  pl.BlockSpec((c, c), vec), pl.BlockSpec((1, c), vec), pl.BlockSpec((1, c), vec)],
        out_specs=[pl.BlockSpec((1, tt, c), seq), pl.BlockSpec((1, 1, c), lambda i, j: (i, 0, 0))],
        out_shape=[jax.ShapeDtypeStruct((b, t, c), F32), jax.ShapeDtypeStruct((b, 1, c), F32)],
        scratch_shapes=[pltpu.VMEM((tt + SUBLANES, c), F32), pltpu.VMEM((tt, c), F32),
                        pltpu.VMEM((tt, c), F32), pltpu.VMEM((SUBLANES, c), F32)],
        compiler_params=_params("parallel", "arbitrary"),
        name="rglru_prompt",
    )(xr, gate, cw, cb, wa, ba, wx, bx, lam)


def _rglru_sample_kernel(xr_ref, gate_ref, cp_ref, h0_ref, cw_ref, cb_ref, wa_ref, ba_ref, wx_ref, bx_ref,
                         lam_ref, y_ref, hlast_ref):
    t_steps = xr_ref.shape[0]
    rows = [cp_ref[i] for i in range(CONV_WIDTH - 1)] + [xr_ref[i] for i in range(t_steps)]
    cw = cw_ref[...]
    h = h0_ref[...]
    for t in range(t_steps):
        xc = cb_ref[...]
        for w in range(CONV_WIDTH):
            xc = xc + cw[w:w + 1, :] * rows[t + w]
        a, u = _lru_gates(xc, wa_ref[...], ba_ref[...], wx_ref[...], bx_ref[...], lam_ref[...])
        h = a * h + u
        y_ref[t] = h * _gelu(gate_ref[t])
    hlast_ref[...] = h


def _rglru_sample(xr_t, gate_t, cp_t, h0, cw, cb, wa, ba, wx, bx, lam):
    t, b, c = xr_t.shape
    return pl.pallas_call(
        _rglru_sample_kernel,
        out_shape=[jax.ShapeDtypeStruct((t, b, c), F32), jax.ShapeDtypeStruct((b, c), F32)],
        compiler_params=pltpu.CompilerParams(vmem_limit_bytes=VMEM_LIMIT),
        name="rglru_sample",
    )(xr_t, gate_t, cp_t, h0, cw, cb, wa, ba, wx, bx, lam)


def _softmax_step(s, m, l, acc, vb):
    mn = jnp.maximum(m, jnp.max(s, axis=-1, keepdims=True))
    alpha = jnp.exp(m - mn)
    p = jnp.exp(s - mn)
    l = alpha * l + jnp.sum(p, axis=-1, keepdims=True)
    acc = alpha * acc + jnp.dot(p.astype(BF16), vb, preferred_element_type=F32)
    return mn, l, acc


def _attn_prompt_kernel(q_ref, k_ref, v_ref, bias_ref, o_ref):
    i = pl.program_id(2)
    tq = q_ref.shape[1]
    nb = bias_ref.shape[0]
    q = q_ref[0]
    even = lax.broadcasted_iota(jnp.int32, (tq, HEAD_PAIR), 1) < HEAD_DIM
    zero = jnp.zeros_like(q)
    q_even = jnp.where(even, q, zero)
    q_odd = jnp.where(even, zero, q)
    nt = (((1,), (1,)), ((), ()))

    def block(jj, carry):
        me, le, ae, mo, lo, ao = carry
        off = pl.multiple_of((i - jj) * tq, tq)
        kb = k_ref[0, pl.ds(off, tq), :]
        vb = v_ref[0, pl.ds(off, tq), :]
        bias = bias_ref[jj]
        se = lax.dot_general(q_even, kb, nt, preferred_element_type=F32) + bias
        so = lax.dot_general(q_odd, kb, nt, preferred_element_type=F32) + bias
        me, le, ae = _softmax_step(se, me, le, ae, vb)
        mo, lo, ao = _softmax_step(so, mo, lo, ao, vb)
        return me, le, ae, mo, lo, ao

    m0 = jnp.full((tq, 1), NEG, F32)
    l0 = jnp.zeros((tq, 1), F32)
    a0 = jnp.zeros((tq, HEAD_PAIR), F32)
    me, le, ae, mo, lo, ao = lax.fori_loop(0, jnp.minimum(i, nb - 1) + 1, block, (m0, l0, a0, m0, l0, a0))
    o_ref[0] = jnp.where(even, ae / le, ao / lo)


def _attn_prompt(q, k, v):
    b, s, _ = q.shape
    tq = LANES
    nb = min(MAX_WINDOW // tq + 1, s // tq)
    r = np.arange(tq)
    delta = (np.arange(nb)[:, None, None] * tq) + r[None, :, None] - r[None, None, :]
    bias = jnp.asarray(_log_multiplicity(delta))
    return pl.pallas_call(
        _attn_prompt_kernel,
        grid=(b, D_ATTN // HEAD_PAIR, s // tq),
        in_specs=[pl.BlockSpec((1, tq, HEAD_PAIR), lambda bi, hp, i: (bi, i, hp)),
                  pl.BlockSpec((1, s, HEAD_PAIR), lambda bi, hp, i: (bi, 0, hp)),
                  pl.BlockSpec((1, s, HEAD_PAIR), lambda bi, hp, i: (bi, 0, hp)),
                  pl.BlockSpec((nb, tq, tq), lambda bi, hp, i: (0, 0, 0))],
        out_specs=pl.BlockSpec((1, tq, HEAD_PAIR), lambda bi, hp, i: (bi, i, hp)),
        out_shape=jax.ShapeDtypeStruct((b, s, D_ATTN), F32),
        compiler_params=_params("parallel", "parallel", "arbitrary"),
        name="attn_prompt",
    )(q, k, v, bias)


SAMPLE_ROWS = 16


def _attn_sample_kernel(q_ref, kc_ref, vc_ref, kn_ref, vn_ref, bias_ref, o_ref):
    t_new = q_ref.shape[1]
    even = lax.broadcasted_iota(jnp.int32, (t_new, HEAD_PAIR), 1) < HEAD_DIM
    row = lax.broadcasted_iota(jnp.int32, (SAMPLE_ROWS, 1), 0)
    qpos = row % t_new
    live = row < 2 * t_new
    nt = (((1,), (1,)), ((), ()))
    bias = bias_ref[...]
    for hp in range(D_ATTN // HEAD_PAIR):
        sl = slice(hp * HEAD_PAIR, (hp + 1) * HEAD_PAIR)
        q = q_ref[0, :, sl]
        zero = jnp.zeros_like(q)
        pad = jnp.zeros((SAMPLE_ROWS - 2 * t_new, HEAD_PAIR), F32)
        q16 = jnp.concatenate([jnp.where(even, q, zero), jnp.where(even, zero, q), pad], axis=0)
        kc = kc_ref[0, :, sl].astype(BF16)
        vc = vc_ref[0, :, sl].astype(BF16)
        s = lax.dot_general(q16.astype(BF16), kc, nt, preferred_element_type=F32) + bias
        kn = kn_ref[0, :, sl]
        vn = vn_ref[0, :, sl]
        ln = []
        for jj in range(t_new):
            l = jnp.sum(q16 * kn[jj:jj + 1, :], axis=-1, keepdims=True)
            for delta in range(t_new - jj):
                lb = float(_log_multiplicity(delta))
                if lb != 0.0:
                    l = l + jnp.where(qpos - jj == delta, np.float32(lb), np.float32(0.0))
            ln.append(jnp.where(live & (qpos >= jj), l, NEG))
        m = jnp.max(s, axis=-1, keepdims=True)
        for l in ln:
            m = jnp.maximum(m, l)
        p = jnp.exp(s - m)
        den = jnp.sum(p, axis=-1, keepdims=True)
        acc = jnp.dot(p.astype(BF16), vc, preferred_element_type=F32)
        for jj in range(t_new):
            pn = jnp.exp(ln[jj] - m)
            den = den + pn
            acc = acc + pn * vn[jj:jj + 1, :]
        o16 = acc / den
        o_ref[0, :, sl] = jnp.where(even, o16[0:t_new, :], o16[t_new:2 * t_new, :])


def _attn_sample(q, k_new, v_new, cache_k, cache_v):
    b, t, _ = q.shape
    p = cache_k.shape[1]
    assert 2 * t <= SAMPLE_ROWS
    jq = np.arange(SAMPLE_ROWS) % t
    delta = p + jq[:, None] - np.arange(p)[None, :]
    bias = jnp.asarray(_log_multiplicity(delta))
    new = pl.BlockSpec((1, t, D_ATTN), lambda i: (i, 0, 0))
    old = pl.BlockSpec((1, p, D_ATTN), lambda i: (i, 0, 0))
    return pl.pallas_call(
        _attn_sample_kernel,
        grid=(b,),
        in_specs=[new, old, old, new, new, pl.BlockSpec((SAMPLE_ROWS, p), lambda i: (0, 0))],
        out_specs=new,
        out_shape=jax.ShapeDtypeStruct((b, t, D_ATTN), F32),
        compiler_params=_params("parallel"),
        name="attn_sample",
    )(q, cache_k, cache_v, k_new, v_new, bias)


def _merge_kernel(yr_ref, ya_ref, x_ref, gr_ref, ga_ref, wo_ref, gf_ref, x1_ref, xt_ref):
    a = _rms(yr_ref[...], gr_ref[...]).astype(BF16)
    b = _rms(ya_ref[...], ga_ref[...]).astype(BF16)
    merged = (jnp.dot(a, wo_ref[0:D_RNN, :], preferred_element_type=F32)
              + jnp.dot(b, wo_ref[D_RNN:D_RNN + D_ATTN, :], preferred_element_type=F32))
    x1 = x_ref[...] + merged
    x1_ref[...] = x1
    xt_ref[...] = _rms(x1, gf_ref[...]).T.astype(BF16)


def _merge(y_rnn, y_attn, x, g_rnn, g_attn, w_out_bf, g_ffn):
    n, d = x.shape
    tm = _tile(n, 512)
    row = lambda i: (i, 0)
    full = lambda i: (0, 0)
    return pl.pallas_call(
        _merge_kernel,
        grid=(n // tm,),
        in_specs=[pl.BlockSpec((tm, D_RNN), row), pl.BlockSpec((tm, D_ATTN), row), pl.BlockSpec((tm, d), row),
                  pl.BlockSpec((1, D_RNN), full), pl.BlockSpec((1, D_ATTN), full),
                  pl.BlockSpec((D_RNN + D_ATTN, d), full), pl.BlockSpec((1, d), full)],
        out_specs=[pl.BlockSpec((tm, d), row), pl.BlockSpec((d, tm), lambda i: (0, i))],
        out_shape=[jax.ShapeDtypeStruct((n, d), F32), jax.ShapeDtypeStruct((d, n), BF16)],
        compiler_params=_params("parallel"),
        name="merge",
    )(y_rnn, y_attn, x, g_rnn, g_attn, w_out_bf, g_ffn)


N_SORT = PEER_TOPK + 1


def _top_sorted(s):
    tn = s.shape[1]
    row = lax.broadcasted_iota(jnp.int32, (SUBLANES, tn), 0)
    blocks = [jnp.full((SUBLANES, tn), NEG, F32) for _ in range(-(-N_SORT // SUBLANES))]
    cur = s
    for k in range(N_SORT):
        mk = jnp.max(cur, axis=0, keepdims=True)
        blocks[k // SUBLANES] = jnp.where(row == k % SUBLANES, mk, blocks[k // SUBLANES])
        if k + 1 < N_SORT:
            cur = jnp.where(cur == mk, NEG, cur)
    return blocks


def _candidates(a_blk, b_blk):
    a0, a1, a2 = a_blk
    b0, b1, b2 = b_blk
    tn = a0.shape[1]
    row = lax.broadcasted_iota(jnp.int32, (SUBLANES, tn), 0)
    out = [a0 + b0[0:1, :], a1 + b0[0:1, :], b1 + a0[0:1, :]]
    for l in range(1, SUBLANES):
        n_k = N_SORT // (l + 1)
        c = a0 + b0[l:l + 1, :]
        out.append(c if n_k >= SUBLANES else jnp.where(row < n_k, c, NEG))
    out.append(jnp.where(row == 0, a2[0:1, :] + b0[0:1, :], jnp.where(row == 1, b2[0:1, :] + a0[0:1, :], NEG)))
    return out


def _route_kernel(xt_ref, wq_ref, keys_ref, thr_ref, e1_ref, s2_ref, e2_ref, q_ref):
    q_ref[...] = jnp.dot(wq_ref[...], xt_ref[...], preferred_element_type=F32).astype(BF16)

    def head(h, carry):
        r1 = pl.multiple_of(h * (2 * PEER_HALF), 2 * PEER_HALF)
        r2 = pl.multiple_of(r1 + PEER_HALF, PEER_HALF)
        s1 = jnp.dot(keys_ref[2 * h], q_ref[pl.ds(r1, PEER_HALF), :], preferred_element_type=F32)
        s2 = jnp.dot(keys_ref[2 * h + 1], q_ref[pl.ds(r2, PEER_HALF), :], preferred_element_type=F32)
        a_blk = _top_sorted(s1)
        b_blk = _top_sorted(s2)
        cands = _candidates(a_blk, b_blk)
        cur = list(cands)
        kth = None
        for k in range(N_SORT):
            mx = functools.reduce(jnp.maximum, cur)
            mx = jnp.max(mx, axis=0, keepdims=True)
            if k == PEER_TOPK - 1:
                kth = mx
            if k + 1 < N_SORT:
                cur = [jnp.where(c == mx, NEG, c) for c in cur]
        tau = 0.5 * (kth + mx)
        m1 = a_blk[0][0:1, :]
        m2 = b_blk[0][0:1, :]
        z = functools.reduce(
            lambda x, y: x + y,
            [jnp.where(c >= tau, jnp.exp(c - (m1 + m2)), 0.0) for c in cands])
        z = jnp.sum(z, axis=0, keepdims=True)
        o = pl.multiple_of(h * N_KEYS, N_KEYS)
        thr_ref[pl.ds(o, N_KEYS), :] = tau - s1
        e1_ref[pl.ds(o, N_KEYS), :] = jnp.exp(s1 - m1) / z
        s2_ref[pl.ds(o, N_KEYS), :] = s2
        e2_ref[pl.ds(o, N_KEYS), :] = jnp.exp(s2 - m2)
        return carry

    lax.fori_loop(0, PEER_HEADS, head, 0)


def _route(xt, wq_t, keys):
    d, n = xt.shape
    tn = _tile(n, 512)
    rows = PEER_HEADS * N_KEYS
    col = lambda i: (0, i)
    o = pl.BlockSpec((rows, tn), col)
    return pl.pallas_call(
        _route_kernel,
        grid=(n // tn,),
        in_specs=[pl.BlockSpec((d, tn), col), pl.BlockSpec(wq_t.shape, lambda i: (0, 0)),
                  pl.BlockSpec(keys.shape, lambda i: (0, 0, 0))],
        out_specs=[o] * 4,
        out_shape=[jax.ShapeDtypeStruct((rows, n), F32)] * 4,
        scratch_shapes=[pltpu.VMEM((wq_t.shape[0], tn), BF16)],
        compiler_params=_params("parallel"),
        name="peer_route",
    )(xt, wq_t, keys)


EXPERT_CHUNK = 512
LANE_BLOCK = 256


def _peer_kernel(xt_ref, u_ref, vt_ref, thr_ref, e1_ref, s2_ref, e2_ref, x1_ref, gfin_ref, y_ref, acc_ref, w_ref):
    c = pl.program_id(1)
    ec = u_ref.shape[0]
    tn = xt_ref.shape[1]
    lb = min(LANE_BLOCK, tn)

    @pl.when(c == 0)
    def _():
        acc_ref[...] = jnp.zeros(acc_ref.shape, F32)

    act = jnp.dot(u_ref[...], xt_ref[...], preferred_element_type=F32)
    for ii in range(ec // N_KEYS):
        i = c * (ec // N_KEYS) + ii
        for b in range(tn // lb):
            ls = slice(b * lb, (b + 1) * lb)
            g = jnp.zeros((N_KEYS, lb), F32)
            for h in range(PEER_HEADS):
                hs = slice(h * N_KEYS, (h + 1) * N_KEYS)
                thr = thr_ref[pl.ds(h * N_KEYS + i, 1), ls]
                e1 = e1_ref[pl.ds(h * N_KEYS + i, 1), ls]
                g = g + jnp.where(s2_ref[hs, ls] >= thr, e2_ref[hs, ls], 0.0) * e1
            a = act[ii * N_KEYS:(ii + 1) * N_KEYS, ls]
            w_ref[ii * N_KEYS:(ii + 1) * N_KEYS, ls] = (g * _gelu(a)).astype(BF16)
    acc_ref[...] += jnp.dot(vt_ref[...], w_ref[...], preferred_element_type=F32)

    @pl.when(c == pl.num_programs(1) - 1)
    def _():
        x2 = x1_ref[...] + acc_ref[...].T
        y_ref[...] = _rms(x2, gfin_ref[...])


def _peer(xt, u_bf, vt_bf, thr, e1, s2, e2, x1, g_final):
    d, n = xt.shape
    n_exp = u_bf.shape[0]
    tn = _tile(n, 512)
    ec = EXPERT_CHUNK
    rows = PEER_HEADS * N_KEYS
    tok = lambda i, c: (0, i)
    sc = pl.BlockSpec((rows, tn), tok)
    return pl.pallas_call(
        _peer_kernel,
        grid=(n // tn, n_exp // ec),
        in_specs=[pl.BlockSpec((d, tn), tok),
                  pl.BlockSpec((ec, d), lambda i, c: (c, 0)),
                  pl.BlockSpec((d, ec), lambda i, c: (0, c)),
                  sc, sc, sc, sc,
                  pl.BlockSpec((tn, d), lambda i, c: (i, 0)),
                  pl.BlockSpec((1, d), lambda i, c: (0, 0))],
        out_specs=pl.BlockSpec((tn, d), lambda i, c: (i, 0)),
        out_shape=jax.ShapeDtypeStruct((n, d), F32),
        scratch_shapes=[pltpu.VMEM((d, tn), F32), pltpu.VMEM((ec, tn), BF16)],
        compiler_params=_params("parallel", "arbitrary"),
        name="peer_experts",
    )(xt, u_bf, vt_bf, thr, e1, s2, e2, x1, g_final)


def _block_diag(w):
    return jax.scipy.linalg.block_diag(*[w[i] for i in range(w.shape[0])])


def kernel(x_prompt, x_sample, cache_k, cache_v, state_conv, state_h, g_mix, w_in, conv_w, conv_b, w_a, b_a, w_x,
           b_x, lru_lambda, g_rnn_out, g_attn_out, w_out, g_ffn, w_query, sub_keys, expert_u, expert_v, g_final):
    depth = g_mix.shape[0]
    assert depth == 1
    bp, sp, d = x_prompt.shape
    bs, ts, _ = x_sample.shape
    assert ts >= CONV_WIDTH - 1
    l = 0
    w_in_bf = w_in[l].astype(BF16)
    w_out_bf = w_out[l].astype(BF16)
    wa = _block_diag(w_a[l]).astype(BF16)
    wx = _block_diag(w_x[l]).astype(BF16)
    row = lambda v: v.reshape(1, -1)
    cw, cb, ba, bx, lam = conv_w[l], row(conv_b[l]), row(b_a[l]), row(b_x[l]), row(lru_lambda[l])
    wq_t = w_query[l].T.astype(BF16)
    keys = sub_keys[l].reshape(2 * PEER_HEADS, N_KEYS, PEER_HALF).astype(BF16)
    u_bf = expert_u[l].astype(BF16)
    vt_bf = expert_v[l].T.astype(BF16)
    gfin = row(g_final)

    def ffn(y_rnn, y_attn, x):
        x1, xt = _merge(y_rnn, y_attn, x, row(g_rnn_out[l]), row(g_attn_out[l]), w_out_bf, row(g_ffn[l]))
        thr, e1, s2, e2 = _route(xt, wq_t, keys)
        return _peer(xt, u_bf, vt_bf, thr, e1, s2, e2, x1, gfin)

    xp = x_prompt.reshape(bp * sp, d)
    xr, gate, q, k, v, kf, vf = _inproj(xp, g_mix[l], w_in_bf)
    seq = lambda a: a.reshape(bp, sp, -1)
    y_rnn, h_p = _rglru_prompt(seq(xr), seq(gate), cw, cb, wa, ba, wx, bx, lam)
    y_attn = _attn_prompt(seq(q), seq(k), seq(v))
    y_p = ffn(y_rnn.reshape(bp * sp, -1), y_attn.reshape(bp * sp, -1), xp).reshape(bp, sp, d)
    keep = min(MAX_WINDOW, sp)
    k_p = seq(kf)[:, sp - keep:].reshape(1, bp, keep, N_HEADS, HEAD_DIM)
    v_p = seq(vf)[:, sp - keep:].reshape(1, bp, keep, N_HEADS, HEAD_DIM)
    conv_p = seq(xr)[:, sp - (CONV_WIDTH - 1):][None]
    h_p = h_p.reshape(1, bp, D_RNN)

    xs = x_sample.reshape(bs * ts, d)
    xr, gate, q, k, v, kf, vf = _inproj(xs, g_mix[l], w_in_bf)
    seq = lambda a: a.reshape(bs, ts, -1)
    tmaj = lambda a: jnp.swapaxes(seq(a), 0, 1)
    y_t, h_s = _rglru_sample(tmaj(xr), tmaj(gate), jnp.swapaxes(state_conv[l], 0, 1), state_h[l],
                             cw, cb, wa, ba, wx, bx, lam)
    y_rnn = jnp.swapaxes(y_t, 0, 1)
    past = cache_k.shape[2]
    y_attn = _attn_sample(seq(q).astype(F32), seq(k).astype(F32), seq(v).astype(F32),
                          cache_k[l].reshape(bs, past, D_ATTN), cache_v[l].reshape(bs, past, D_ATTN))
    y_s = ffn(y_rnn.reshape(bs * ts, -1), y_attn.reshape(bs * ts, -1), xs).reshape(bs, ts, d)
    k_s = seq(kf).reshape(1, bs, ts, N_HEADS, HEAD_DIM)
    v_s = seq(vf).reshape(1, bs, ts, N_HEADS, HEAD_DIM)
    conv_s = seq(xr)[:, ts - (CONV_WIDTH - 1):][None]
    h_s = h_s.reshape(1, bs, D_RNN)

    return (y_p, y_s, k_p, v_p, k_s, v_s, conv_p, conv_s, h_p, h_s)
```

```python
import functools

import numpy as np
import jax
import jax.numpy as jnp
from jax import lax
from jax.experimental import pallas as pl
from jax.experimental.pallas import tpu as pltpu

F32 = jnp.float32
BF16 = jnp.bfloat16

EPS = 1e-6
D_RNN = 512
D_ATTN = 512
N_HEADS = 8
HEAD_DIM = 64
CONV_WIDTH = 4
LRU_C = 8.0
DILATED_PATTERNS = ((128, 1), (512, 4), (2048, 16))
MAX_WINDOW = 2048
N_KEYS = 128
PEER_HEADS = 8
PEER_HALF = 128
PEER_TOPK = 16

LANES = 128
SUBLANES = 8
HEAD_PAIR = LANES
NEG = -1e30
VMEM_LIMIT = 48 * 1024 * 1024


def _params(*sem):
    return pltpu.CompilerParams(dimension_semantics=sem, vmem_limit_bytes=VMEM_LIMIT)


def _tile(n, pref):
    return pref if n % pref == 0 else n


def _rms(x, g):
    return x * lax.rsqrt(jnp.mean(x * x, axis=-1, keepdims=True) + EPS) * g


def _gelu(x):
    c = np.float32(np.sqrt(2.0 / np.pi))
    return x * (0.5 * (1.0 + jnp.tanh(c * (x + 0.044715 * (x * x * x)))))


def _sigmoid(x):
    return 1.0 / (1.0 + jnp.exp(-x))


def _multiplicity(delta):
    delta = np.asarray(delta)
    cnt = np.zeros(delta.shape, np.int32)
    for window, dil in DILATED_PATTERNS:
        cnt += ((delta >= 0) & (delta <= window) & (delta % dil == 0)).astype(np.int32)
    return cnt


def _log_multiplicity(delta, same=None):
    cnt = jnp.zeros(delta.shape, F32)
    for window, dil in DILATED_PATTERNS:
        cnt = cnt + ((delta >= 0) & (delta <= window) & (delta % dil == 0)).astype(F32)
    ok = cnt > 0
    if same is not None:
        ok = ok & same
    return jnp.where(ok, jnp.log(jnp.maximum(cnt, 1.0)), NEG)


def _inproj_kernel(x_ref, g_ref, w_ref, xr_ref, gate_ref, q_ref, k_ref, v_ref, kf_ref, vf_ref):
    xn = _rms(x_ref[...], g_ref[...])
    z = jnp.dot(xn.astype(BF16), w_ref[...], preferred_element_type=F32)
    xr_ref[...] = z[:, 0:D_RNN]
    gate_ref[...] = z[:, D_RNN:2 * D_RNN]
    o = 2 * D_RNN
    q_ref[...] = (z[:, o:o + D_ATTN] * (HEAD_DIM ** -0.5)).astype(BF16)
    k = z[:, o + D_ATTN:o + 2 * D_ATTN]
    v = z[:, o + 2 * D_ATTN:o + 3 * D_ATTN]
    k_ref[...] = k.astype(BF16)
    v_ref[...] = v.astype(BF16)
    kf_ref[...] = k
    vf_ref[...] = v


def _inproj(x, g_mix, w_in_bf):
    n, d = x.shape
    d_in = w_in_bf.shape[1]
    tm = _tile(n, 512)
    row = lambda i: (i, 0)
    full = lambda i: (0, 0)
    o512 = pl.BlockSpec((tm, 512), row)
    return pl.pallas_call(
        _inproj_kernel,
        grid=(n // tm,),
        in_specs=[pl.BlockSpec((tm, d), row), pl.BlockSpec((1, d), full), pl.BlockSpec((d, d_in), full)],
        out_specs=[o512] * 7,
        out_shape=[jax.ShapeDtypeStruct((n, 512), F32), jax.ShapeDtypeStruct((n, 512), F32),
                   jax.ShapeDtypeStruct((n, 512), BF16), jax.ShapeDtypeStruct((n, 512), BF16),
                   jax.ShapeDtypeStruct((n, 512), BF16), jax.ShapeDtypeStruct((n, 512), F32),
                   jax.ShapeDtypeStruct((n, 512), F32)],
        compiler_params=_params("parallel"),
        name="inproj",
    )(x, g_mix.reshape(1, d), w_in_bf)


def _lru_gates(xc, wa, ba, wx, bx, lam):
    xb = xc.astype(BF16)
    r = _sigmoid(jnp.dot(xb, wa, preferred_element_type=F32) + ba)
    i = _sigmoid(jnp.dot(xb, wx, preferred_element_type=F32) + bx)
    nl = -lam
    softplus = jnp.maximum(nl, 0.0) + jnp.log1p(jnp.exp(-jnp.abs(nl)))
    log_a = (-LRU_C) * r * softplus
    a = jnp.exp(log_a)
    u = jnp.sqrt(-jnp.tanh(log_a) * (a * a + 1.0)) * (i * xc)
    return a, u


def _rglru_prompt_kernel(xr_ref, gate_ref, cw_ref, cb_ref, wa_ref, ba_ref, wx_ref, bx_ref, lam_ref,
                         y_ref, hlast_ref, ext_ref, a_ref, u_ref, hc_ref):
    t = pl.program_id(1)
    tt = xr_ref.shape[1]

    @pl.when(t == 0)
    def _():
        ext_ref[0:SUBLANES, :] = jnp.zeros((SUBLANES, D_RNN), F32)
        hc_ref[...] = jnp.zeros(hc_ref.shape, F32)

    ext_ref[SUBLANES:SUBLANES + tt, :] = xr_ref[0]
    cw = cw_ref[...]
    xc = cb_ref[...]
    for w in range(CONV_WIDTH):
        lo = SUBLANES - (CONV_WIDTH - 1) + w
        xc = xc + cw[w:w + 1, :] * ext_ref[lo:lo + tt, :]
    ext_ref[0:SUBLANES, :] = ext_ref[tt:tt + SUBLANES, :]

    a, u = _lru_gates(xc, wa_ref[...], ba_ref[...], wx_ref[...], bx_ref[...], lam_ref[...])
    a_ref[...] = a
    u_ref[...] = u

    row = lax.broadcasted_iota(jnp.int32, (SUBLANES, D_RNN), 0)

    def group(g, hc):
        off = pl.multiple_of(g * SUBLANES, SUBLANES)
        a8 = a_ref[pl.ds(off, SUBLANES), :]
        u8 = u_ref[pl.ds(off, SUBLANES), :]
        for s in (1, 2, 4):
            a_sh = pltpu.roll(a8, s, 0)
            u_sh = pltpu.roll(u8, s, 0)
            m = row >= s
            u8 = jnp.where(m, u8 + a8 * u_sh, u8)
            a8 = jnp.where(m, a8 * a_sh, a8)
        h8 = u8 + a8 * hc
        u_ref[pl.ds(off, SUBLANES), :] = h8
        return h8[SUBLANES - 1:SUBLANES, :]

    hc = lax.fori_loop(0, tt // SUBLANES, group, hc_ref[0:1, :])
    hc_ref[...] = jnp.broadcast_to(hc, hc_ref.shape)
    hlast_ref[0] = hc
    y_ref[0] = u_ref[...] * _gelu(gate_ref[0])


def _rglru_prompt(xr, gate, cw, cb, wa, ba, wx, bx, lam):
    b, t, c = xr.shape
    tt = _tile(t, 512)
    seq = lambda i, j: (i, j, 0)
    vec = lambda i, j: (0, 0)
    return pl.pallas_call(
        _rglru_prompt_kernel,
        grid=(b, t // tt),
        in_specs=[pl.BlockSpec((1, tt, c), seq), pl.BlockSpec((1, tt, c), seq),
                  pl.BlockSpec((CONV_WIDTH, c), vec), pl.BlockSpec((1, c), vec),
                  pl.BlockSpec((c, c), vec), pl.BlockSpec((1, c), vec),
                  pl.BlockSpec((c, c), vec), pl.BlockSpec((1, c), vec), pl.BlockSpec((1, c), vec)],
        out_specs=[pl.BlockSpec((1, tt, c), seq), pl.BlockSpec((1, 1, c), lambda i, j: (i, 0, 0))],
        out_shape=[jax.ShapeDtypeStruct((b, t, c), F32), jax.ShapeDtypeStruct((b, 1, c), F32)],
        scratch_shapes=[pltpu.VMEM((tt + SUBLANES, c), F32), pltpu.VMEM((tt, c), F32),
                        pltpu.VMEM((tt, c), F32), pltpu.VMEM((SUBLANES, c), F32)],
        compiler_params=_params("parallel", "arbitrary"),
        name="rglru_prompt",
    )(xr, gate, cw, cb, wa, ba, wx, bx, lam)


def _rglru_sample_kernel(xr_ref, gate_ref, cp_ref, h0_ref, cw_ref, cb_ref, wa_ref, ba_ref, wx_ref, bx_ref,
                         lam_ref, y_ref, hlast_ref):
    t_steps = xr_ref.shape[0]
    rows = [cp_ref[i] for i in range(CONV_WIDTH - 1)] + [xr_ref[i] for i in range(t_steps)]
    cw = cw_ref[...]
    h = h0_ref[...]
    for t in range(t_steps):
        xc = cb_ref[...]
        for w in range(CONV_WIDTH):
            xc = xc + cw[w:w + 1, :] * rows[t + w]
        a, u = _lru_gates(xc, wa_ref[...], ba_ref[...], wx_ref[...], bx_ref[...], lam_ref[...])
        h = a * h + u
        y_ref[t] = h * _gelu(gate_ref[t])
    hlast_ref[...] = h


def _rglru_sample(xr_t, gate_t, cp_t, h0, cw, cb, wa, ba, wx, bx, lam):
    t, b, c = xr_t.shape
    return pl.pallas_call(
        _rglru_sample_kernel,
        out_shape=[jax.ShapeDtypeStruct((t, b, c), F32), jax.ShapeDtypeStruct((b, c), F32)],
        compiler_params=pltpu.CompilerParams(vmem_limit_bytes=VMEM_LIMIT),
        name="rglru_sample",
    )(xr_t, gate_t, cp_t, h0, cw, cb, wa, ba, wx, bx, lam)


ATTN_TQ = 256
NT_DIMS = (((1,), (1,)), ((), ()))


def _attn_prompt_kernel(q_ref, k_ref, v_ref, bias_ref, o_ref):
    i = pl.program_id(2)
    tq = q_ref.shape[1]
    w = bias_ref.shape[2]
    start = pl.multiple_of(jnp.maximum((i + 1) * tq - w, 0), LANES)
    kw = k_ref[0, pl.ds(start, w), :]
    vw = v_ref[0, pl.ds(start, w), :]
    bias = bias_ref[0]
    q = q_ref[0]
    even = lax.broadcasted_iota(jnp.int32, (tq, HEAD_PAIR), 1) < HEAD_DIM
    zero = jnp.zeros_like(q)

    def head(qh):
        s = lax.dot_general(qh, kw, NT_DIMS, preferred_element_type=F32) + bias
        m = jnp.max(s, axis=-1, keepdims=True)
        p = jnp.exp(s - m)
        den = jnp.sum(p, axis=-1, keepdims=True)
        return jnp.dot(p.astype(BF16), vw, preferred_element_type=F32) / den

    o_ref[0] = jnp.where(even, head(jnp.where(even, q, zero)), head(jnp.where(even, zero, q)))


def _attn_prompt(q, k, v):
    b, s, _ = q.shape
    tq = _tile(s, ATTN_TQ)
    w = min(s, MAX_WINDOW + tq)
    assert w % tq == 0 and w % LANES == 0
    n_off = w // tq
    ti = lax.broadcasted_iota(jnp.int32, (n_off, tq, w), 0)
    r = lax.broadcasted_iota(jnp.int32, (n_off, tq, w), 1)
    c = lax.broadcasted_iota(jnp.int32, (n_off, tq, w), 2)
    bias = _log_multiplicity(ti * tq + r - c)
    return pl.pallas_call(
        _attn_prompt_kernel,
        grid=(b, D_ATTN // HEAD_PAIR, s // tq),
        in_specs=[pl.BlockSpec((1, tq, HEAD_PAIR), lambda bi, hp, i: (bi, i, hp)),
                  pl.BlockSpec((1, s, HEAD_PAIR), lambda bi, hp, i: (bi, 0, hp)),
                  pl.BlockSpec((1, s, HEAD_PAIR), lambda bi, hp, i: (bi, 0, hp)),
                  pl.BlockSpec((1, tq, w), lambda bi, hp, i: (jnp.minimum(i, n_off - 1), 0, 0))],
        out_specs=pl.BlockSpec((1, tq, HEAD_PAIR), lambda bi, hp, i: (bi, i, hp)),
        out_shape=jax.ShapeDtypeStruct((b, s, D_ATTN), F32),
        compiler_params=_params("parallel", "parallel", "arbitrary"),
        name="attn_prompt",
    )(q, k, v, bias)


MAX_DILATION = max(d for _, d in DILATED_PATTERNS)
NEAR_WINDOW = max(w for w, d in DILATED_PATTERNS if d < MAX_DILATION)


def _attn_sample_kernel(q_ref, kf_ref, kr_ref, vf_ref, vr_ref, kn_ref, vn_ref, bf_ref, br_ref, bn_ref, o_ref):
    q = q_ref[0]

    def flat(ref):
        x = ref[0]
        return x.reshape(-1, HEAD_DIM).astype(BF16)

    ks = (flat(kf_ref), flat(kr_ref), kn_ref[0])
    vs = (flat(vf_ref), flat(vr_ref), vn_ref[0])
    bs = (bf_ref[...], br_ref[...], bn_ref[...])
    ss = [lax.dot_general(q, k, NT_DIMS, preferred_element_type=F32) + b for k, b in zip(ks, bs)]
    m = functools.reduce(jnp.maximum, [jnp.max(s, axis=-1, keepdims=True) for s in ss])
    ps = [jnp.exp(s - m) for s in ss]
    den = functools.reduce(lambda x, y: x + y, [jnp.sum(p, axis=-1, keepdims=True) for p in ps])
    acc = functools.reduce(lambda x, y: x + y,
                           [jnp.dot(p.astype(BF16), v, preferred_element_type=F32) for p, v in zip(ps, vs)])
    o_ref[0] = acc / den


def _attn_sample(q, k_new, v_new, cache_k, cache_v):
    b, t, _ = q.shape
    p = cache_k.shape[1]
    g = MAX_DILATION
    n_recent = NEAR_WINDOW // g
    n_far = p // g - n_recent
    assert p % g == 0 and NEAR_WINDOW % g == 0 and n_far > 0 and n_far % n_recent == 0 and t <= g
    pos = np.arange(p)
    need = np.zeros(p, bool)
    for j in range(t):
        need |= _multiplicity(p + j - pos) > 0
    have = (pos // g >= n_far) | (pos % g < t)
    assert not (need & ~have).any()

    rows = t * N_HEADS
    heads = lambda a: a.reshape(b, rows, HEAD_DIM)
    grp = lambda a: a.reshape(b, p // g, g, N_HEADS, HEAD_DIM)

    jq = lax.broadcasted_iota(jnp.int32, (rows, 1), 0) // N_HEADS
    hq = lax.broadcasted_iota(jnp.int32, (rows, 1), 0) % N_HEADS

    def bias(positions):
        n = positions.shape[0]
        kp = jnp.repeat(positions, N_HEADS)[None, :]
        kh = jnp.tile(jnp.arange(N_HEADS, dtype=jnp.int32), n)[None, :]
        return _log_multiplicity(p + jq - kp, same=(hq == kh))

    gi = jnp.arange(n_far, dtype=jnp.int32)[:, None] * g + jnp.arange(t, dtype=jnp.int32)[None, :]
    b_far = bias(gi.reshape(-1))
    b_recent = bias(n_far * g + jnp.arange(n_recent * g, dtype=jnp.int32))
    b_new = bias(p + jnp.arange(t, dtype=jnp.int32))

    new = pl.BlockSpec((1, rows, HEAD_DIM), lambda i: (i, 0, 0))
    far = pl.BlockSpec((1, n_far, t, N_HEADS, HEAD_DIM), lambda i: (i, 0, 0, 0, 0))
    recent = pl.BlockSpec((1, n_recent, g, N_HEADS, HEAD_DIM), lambda i: (i, n_far // n_recent, 0, 0, 0))
    const = lambda a: pl.BlockSpec(a.shape, lambda i: (0, 0))
    out = pl.pallas_call(
        _attn_sample_kernel,
        grid=(b,),
        in_specs=[new, far, recent, far, recent, new, new, const(b_far), const(b_recent), const(b_new)],
        out_specs=new,
        out_shape=jax.ShapeDtypeStruct((b, rows, HEAD_DIM), F32),
        compiler_params=_params("parallel"),
        name="attn_sample",
    )(heads(q), grp(cache_k), grp(cache_k), grp(cache_v), grp(cache_v), heads(k_new), heads(v_new),
      b_far, b_recent, b_new)
    return out.reshape(b, t, D_ATTN)


def _merge_kernel(yr_ref, ya_ref, x_ref, gr_ref, ga_ref, wo_ref, gf_ref, x1_ref, xt_ref):
    a = _rms(yr_ref[...], gr_ref[...]).astype(BF16)
    b = _rms(ya_ref[...], ga_ref[...]).astype(BF16)
    merged = (jnp.dot(a, wo_ref[0:D_RNN, :], preferred_element_type=F32)
              + jnp.dot(b, wo_ref[D_RNN:D_RNN + D_ATTN, :], preferred_element_type=F32))
    x1 = x_ref[...] + merged
    x1_ref[...] = x1
    xt_ref[...] = _rms(x1, gf_ref[...]).T.astype(BF16)


def _merge(y_rnn, y_attn, x, g_rnn, g_attn, w_out_bf, g_ffn):
    n, d = x.shape
    tm = _tile(n, 512)
    row = lambda i: (i, 0)
    full = lambda i: (0, 0)
    return pl.pallas_call(
        _merge_kernel,
        grid=(n // tm,),
        in_specs=[pl.BlockSpec((tm, D_RNN), row), pl.BlockSpec((tm, D_ATTN), row), pl.BlockSpec((tm, d), row),
                  pl.BlockSpec((1, D_RNN), full), pl.BlockSpec((1, D_ATTN), full),
                  pl.BlockSpec((D_RNN + D_ATTN, d), full), pl.BlockSpec((1, d), full)],
        out_specs=[pl.BlockSpec((tm, d), row), pl.BlockSpec((d, tm), lambda i: (0, i))],
        out_shape=[jax.ShapeDtypeStruct((n, d), F32), jax.ShapeDtypeStruct((d, n), BF16)],
        compiler_params=_params("parallel"),
        name="merge",
    )(y_rnn, y_attn, x, g_rnn, g_attn, w_out_bf, g_ffn)


N_SORT = PEER_TOPK + 1
ROUTE_LANES = 4 * LANES


def _top_sorted(s, with_rank):
    tn = s.shape[1]
    row = lax.broadcasted_iota(jnp.int32, (SUBLANES, tn), 0)
    blocks = [jnp.full((SUBLANES, tn), NEG, F32) for _ in range(-(-N_SORT // SUBLANES))]
    rank = jnp.full(s.shape, float(PEER_TOPK), F32) if with_rank else None
    cur = s
    for k in range(N_SORT):
        mk = jnp.max(cur, axis=0, keepdims=True)
        blocks[k // SUBLANES] = jnp.where(row == k % SUBLANES, mk, blocks[k // SUBLANES])
        if k + 1 < N_SORT:
            hit = cur == mk
            if with_rank:
                rank = jnp.where(hit, float(k), rank)
            cur = jnp.where(hit, NEG, cur)
    return blocks, rank


def _candidates(a_blk, b_blk):
    a0, a1, a2 = a_blk
    b0, b1, b2 = b_blk
    tn = a0.shape[1]
    row = lax.broadcasted_iota(jnp.int32, (SUBLANES, tn), 0)
    out = [a0 + b0[0:1, :], a1 + b0[0:1, :], b1 + a0[0:1, :]]
    for l in range(1, SUBLANES):
        n_k = N_SORT // (l + 1)
        c = a0 + b0[l:l + 1, :]
        out.append(c if n_k >= SUBLANES else jnp.where(row < n_k, c, NEG))
    out.append(jnp.where(row == 0, a2[0:1, :] + b0[0:1, :], jnp.where(row == 1, b2[0:1, :] + a0[0:1, :], NEG)))
    return out


def _route_kernel(xt_ref, wq_ref, keys_ref, n_ref, e1_ref, rank_ref, e2_ref, q_ref):
    q_ref[...] = jnp.dot(wq_ref[...], xt_ref[...], preferred_element_type=F32).astype(BF16)

    def head(h, ls):
        r1 = pl.multiple_of(h * (2 * PEER_HALF), 2 * PEER_HALF)
        r2 = pl.multiple_of(r1 + PEER_HALF, PEER_HALF)
        s1 = jnp.dot(keys_ref[2 * h], q_ref[pl.ds(r1, PEER_HALF), ls], preferred_element_type=F32)
        s2 = jnp.dot(keys_ref[2 * h + 1], q_ref[pl.ds(r2, PEER_HALF), ls], preferred_element_type=F32)
        a_blk, _ = _top_sorted(s1, False)
        b_blk, rank2 = _top_sorted(s2, True)
        cands = _candidates(a_blk, b_blk)
        cur = list(cands)
        kth = None
        for k in range(N_SORT):
            mx = functools.reduce(jnp.maximum, cur)
            mx = jnp.max(mx, axis=0, keepdims=True)
            if k == PEER_TOPK - 1:
                kth = mx
            if k + 1 < N_SORT:
                cur = [jnp.where(c == mx, NEG, c) for c in cur]
        tau = 0.5 * (kth + mx)
        m1 = a_blk[0][0:1, :]
        m2 = b_blk[0][0:1, :]
        z = functools.reduce(
            lambda x, y: x + y,
            [jnp.where(c >= tau, jnp.exp(c - (m1 + m2)), 0.0) for c in cands])
        z = jnp.sum(z, axis=0, keepdims=True)
        thr = tau - s1
        n = jnp.zeros(s1.shape, F32)
        for l in range(PEER_TOPK):
            b_l = b_blk[l // SUBLANES][l % SUBLANES:l % SUBLANES + 1, :]
            n = n + jnp.where(b_l >= thr, 1.0, 0.0)
        o = pl.multiple_of(h * N_KEYS, N_KEYS)
        n_ref[pl.ds(o, N_KEYS), ls] = n
        e1_ref[pl.ds(o, N_KEYS), ls] = jnp.exp(s1 - m1) / z
        rank_ref[pl.ds(o, N_KEYS), ls] = rank2
        e2_ref[pl.ds(o, N_KEYS), ls] = jnp.exp(s2 - m2)

    tn = xt_ref.shape[1]
    lb = min(tn, ROUTE_LANES)
    for b in range(tn // lb):
        ls = slice(b * lb, (b + 1) * lb)

        def body(h, carry, ls=ls):
            head(h, ls)
            return carry

        lax.fori_loop(0, PEER_HEADS, body, 0)


def _route(xt, wq_t, keys):
    d, n = xt.shape
    tn = _tile(n, 512)
    rows = PEER_HEADS * N_KEYS
    col = lambda i: (0, i)
    o = pl.BlockSpec((rows, tn), col)
    return pl.pallas_call(
        _route_kernel,
        grid=(n // tn,),
        in_specs=[pl.BlockSpec((d, tn), col), pl.BlockSpec(wq_t.shape, lambda i: (0, 0)),
                  pl.BlockSpec(keys.shape, lambda i: (0, 0, 0))],
        out_specs=[o] * 4,
        out_shape=[jax.ShapeDtypeStruct((rows, n), F32)] * 4,
        scratch_shapes=[pltpu.VMEM((wq_t.shape[0], tn), BF16)],
        compiler_params=_params("parallel"),
        name="peer_route",
    )(xt, wq_t, keys)


EXPERT_CHUNK = 512


def _peer_kernel(xt_ref, u_ref, vt_ref, n_ref, e1_ref, rank_ref, e2_ref, x1_ref, gfin_ref, y_ref, acc_ref, w_ref):
    c = pl.program_id(1)
    ec = u_ref.shape[0]
    tn = xt_ref.shape[1]
    n_i = ec // N_KEYS

    @pl.when(c == 0)
    def _():
        acc_ref[...] = jnp.zeros(acc_ref.shape, F32)

    act = jnp.dot(u_ref[...], xt_ref[...], preferred_element_type=F32)
    rows = [[(n_ref[pl.ds(h * N_KEYS + c * n_i + ii, 1), :].astype(BF16),
              e1_ref[pl.ds(h * N_KEYS + c * n_i + ii, 1), :].astype(BF16))
             for ii in range(n_i)] for h in range(PEER_HEADS)]
    for b in range(tn // LANES):
        ls = slice(b * LANES, (b + 1) * LANES)
        gs = [jnp.zeros((N_KEYS, LANES), BF16) for _ in range(n_i)]
        for h in range(PEER_HEADS):
            hs = slice(h * N_KEYS, (h + 1) * N_KEYS)
            rank = rank_ref[hs, ls].astype(BF16)
            e2 = e2_ref[hs, ls].astype(BF16)
            zero = jnp.zeros_like(e2)
            for ii in range(n_i):
                n_row, e1_row = rows[h][ii]
                gs[ii] = gs[ii] + jnp.where(rank < n_row[:, ls], e2, zero) * e1_row[:, ls]
        for ii in range(n_i):
            rs = slice(ii * N_KEYS, (ii + 1) * N_KEYS)
            w_ref[rs, ls] = gs[ii] * _gelu(act[rs, ls]).astype(BF16)
    acc_ref[...] += jnp.dot(vt_ref[...], w_ref[...], preferred_element_type=F32)

    @pl.when(c == pl.num_programs(1) - 1)
    def _():
        x2 = x1_ref[...] + acc_ref[...].T
        y_ref[...] = _rms(x2, gfin_ref[...])


def _peer(xt, u_bf, vt_bf, n, e1, rank, e2, x1, g_final):
    d, n_tok = xt.shape
    n_exp = u_bf.shape[0]
    tn = _tile(n_tok, 512)
    ec = EXPERT_CHUNK
    rows = PEER_HEADS * N_KEYS
    tok = lambda i, c: (0, i)
    sc = pl.BlockSpec((rows, tn), tok)
    return pl.pallas_call(
        _peer_kernel,
        grid=(n_tok // tn, n_exp // ec),
        in_specs=[pl.BlockSpec((d, tn), tok),
                  pl.BlockSpec((ec, d), lambda i, c: (c, 0)),
                  pl.BlockSpec((d, ec), lambda i, c: (0, c)),
                  sc, sc, sc, sc,
                  pl.BlockSpec((tn, d), lambda i, c: (i, 0)),
                  pl.BlockSpec((1, d), lambda i, c: (0, 0))],
        out_specs=pl.BlockSpec((tn, d), lambda i, c: (i, 0)),
        out_shape=jax.ShapeDtypeStruct((n_tok, d), F32),
        scratch_shapes=[pltpu.VMEM((d, tn), F32), pltpu.VMEM((ec, tn), BF16)],
        compiler_params=_params("parallel", "arbitrary"),
        name="peer_experts",
    )(xt, u_bf, vt_bf, n, e1, rank, e2, x1, g_final)


def _block_diag(w):
    return jax.scipy.linalg.block_diag(*[w[i] for i in range(w.shape[0])])


def kernel(x_prompt, x_sample, cache_k, cache_v, state_conv, state_h, g_mix, w_in, conv_w, conv_b, w_a, b_a, w_x,
           b_x, lru_lambda, g_rnn_out, g_attn_out, w_out, g_ffn, w_query, sub_keys, expert_u, expert_v, g_final):
    depth = g_mix.shape[0]
    assert depth == 1
    bp, sp, d = x_prompt.shape
    bs, ts, _ = x_sample.shape
    assert ts >= CONV_WIDTH - 1
    l = 0
    w_in_bf = w_in[l].astype(BF16)
    w_out_bf = w_out[l].astype(BF16)
    wa = _block_diag(w_a[l]).astype(BF16)
    wx = _block_diag(w_x[l]).astype(BF16)
    row = lambda v: v.reshape(1, -1)
    cw, cb, ba, bx, lam = conv_w[l], row(conv_b[l]), row(b_a[l]), row(b_x[l]), row(lru_lambda[l])
    wq_t = w_query[l].T.astype(BF16)
    keys = sub_keys[l].reshape(2 * PEER_HEADS, N_KEYS, PEER_HALF).astype(BF16)
    u_bf = expert_u[l].astype(BF16)
    vt_bf = expert_v[l].T.astype(BF16)
    gfin = row(g_final)

    def ffn(y_rnn, y_attn, x):
        x1, xt = _merge(y_rnn, y_attn, x, row(g_rnn_out[l]), row(g_attn_out[l]), w_out_bf, row(g_ffn[l]))
        n, e1, rank, e2 = _route(xt, wq_t, keys)
        return _peer(xt, u_bf, vt_bf, n, e1, rank, e2, x1, gfin)

    xp = x_prompt.reshape(bp * sp, d)
    xr, gate, q, k, v, kf, vf = _inproj(xp, g_mix[l], w_in_bf)
    seq = lambda a: a.reshape(bp, sp, -1)
    y_rnn, h_p = _rglru_prompt(seq(xr), seq(gate), cw, cb, wa, ba, wx, bx, lam)
    y_attn = _attn_prompt(seq(q), seq(k), seq(v))
    y_p = ffn(y_rnn.reshape(bp * sp, -1), y_attn.reshape(bp * sp, -1), xp).reshape(bp, sp, d)
    keep = min(MAX_WINDOW, sp)
    k_p = seq(kf)[:, sp - keep:].reshape(1, bp, keep, N_HEADS, HEAD_DIM)
    v_p = seq(vf)[:, sp - keep:].reshape(1, bp, keep, N_HEADS, HEAD_DIM)
    conv_p = seq(xr)[:, sp - (CONV_WIDTH - 1):][None]
    h_p = h_p.reshape(1, bp, D_RNN)

    xs = x_sample.reshape(bs * ts, d)
    xr, gate, q, k, v, kf, vf = _inproj(xs, g_mix[l], w_in_bf)
    seq = lambda a: a.reshape(bs, ts, -1)
    tmaj = lambda a: jnp.swapaxes(seq(a), 0, 1)
    y_t, h_s = _rglru_sample(tmaj(xr), tmaj(gate), jnp.swapaxes(state_conv[l], 0, 1), state_h[l],
                             cw, cb, wa, ba, wx, bx, lam)
    y_rnn = jnp.swapaxes(y_t, 0, 1)
    y_attn = _attn_sample(seq(q), seq(k), seq(v), cache_k[l], cache_v[l])
    y_s = ffn(y_rnn.reshape(bs * ts, -1), y_attn.reshape(bs * ts, -1), xs).reshape(bs, ts, d)
    k_s = seq(kf).reshape(1, bs, ts, N_HEADS, HEAD_DIM)
    v_s = seq(vf).reshape(1, bs, ts, N_HEADS, HEAD_DIM)
    conv_s = seq(xr)[:, ts - (CONV_WIDTH - 1):][None]
    h_s = h_s.reshape(1, bs, D_RNN)

    return (y_p, y_s, k_p, v_p, k_s, v_s, conv_p, conv_s, h_p, h_s)
```

```python
import functools

import numpy as np
import jax
import jax.numpy as jnp
from jax import lax
from jax.experimental import pallas as pl
from jax.experimental.pallas import tpu as pltpu

F32 = jnp.float32
BF16 = jnp.bfloat16

EPS = 1e-6
D_RNN = 512
D_ATTN = 512
N_HEADS = 8
HEAD_DIM = 64
CONV_WIDTH = 4
LRU_C = 8.0
DILATED_PATTERNS = ((128, 1), (512, 4), (2048, 16))
MAX_WINDOW = 2048
N_KEYS = 128
PEER_HEADS = 8
PEER_HALF = 128
PEER_TOPK = 16

LANES = 128
SUBLANES = 8
HEAD_PAIR = LANES
NEG = -1e30
VMEM_LIMIT = 48 * 1024 * 1024


def _params(*sem):
    return pltpu.CompilerParams(dimension_semantics=sem, vmem_limit_bytes=VMEM_LIMIT)


def _tile(n, pref):
    return pref if n % pref == 0 else n


def _rms(x, g):
    return x * lax.rsqrt(jnp.mean(x * x, axis=-1, keepdims=True) + EPS) * g


def _gelu(x):
    c = np.float32(np.sqrt(2.0 / np.pi))
    return x * (0.5 * (1.0 + jnp.tanh(c * (x + 0.044715 * (x * x * x)))))


def _sigmoid(x):
    return 1.0 / (1.0 + jnp.exp(-x))


def _multiplicity(delta):
    delta = np.asarray(delta)
    cnt = np.zeros(delta.shape, np.int32)
    for window, dil in DILATED_PATTERNS:
        cnt += ((delta >= 0) & (delta <= window) & (delta % dil == 0)).astype(np.int32)
    return cnt


def _log_multiplicity(delta, same=None):
    cnt = jnp.zeros(delta.shape, F32)
    for window, dil in DILATED_PATTERNS:
        cnt = cnt + ((delta >= 0) & (delta <= window) & (delta % dil == 0)).astype(F32)
    ok = cnt > 0
    if same is not None:
        ok = ok & same
    return jnp.where(ok, jnp.log(jnp.maximum(cnt, 1.0)), NEG)


def _inproj_kernel(x_ref, g_ref, w_ref, xr_ref, gate_ref, q_ref, k_ref, v_ref, kf_ref, vf_ref):
    xn = _rms(x_ref[...], g_ref[...])
    z = jnp.dot(xn.astype(BF16), w_ref[...], preferred_element_type=F32)
    xr_ref[...] = z[:, 0:D_RNN]
    gate_ref[...] = z[:, D_RNN:2 * D_RNN]
    o = 2 * D_RNN
    q_ref[...] = (z[:, o:o + D_ATTN] * (HEAD_DIM ** -0.5)).astype(BF16)
    k = z[:, o + D_ATTN:o + 2 * D_ATTN]
    v = z[:, o + 2 * D_ATTN:o + 3 * D_ATTN]
    k_ref[...] = k.astype(BF16)
    v_ref[...] = v.astype(BF16)
    kf_ref[...] = k
    vf_ref[...] = v


def _inproj(x, g_mix, w_in_bf):
    n, d = x.shape
    d_in = w_in_bf.shape[1]
    tm = _tile(n, 512)
    row = lambda i: (i, 0)
    full = lambda i: (0, 0)
    o512 = pl.BlockSpec((tm, 512), row)
    return pl.pallas_call(
        _inproj_kernel,
        grid=(n // tm,),
        in_specs=[pl.BlockSpec((tm, d), row), pl.BlockSpec((1, d), full), pl.BlockSpec((d, d_in), full)],
        out_specs=[o512] * 7,
        out_shape=[jax.ShapeDtypeStruct((n, 512), F32), jax.ShapeDtypeStruct((n, 512), F32),
                   jax.ShapeDtypeStruct((n, 512), BF16), jax.ShapeDtypeStruct((n, 512), BF16),
                   jax.ShapeDtypeStruct((n, 512), BF16), jax.ShapeDtypeStruct((n, 512), F32),
                   jax.ShapeDtypeStruct((n, 512), F32)],
        compiler_params=_params("parallel"),
        name="inproj",
    )(x, g_mix.reshape(1, d), w_in_bf)


def _lru_gates(xc, wa, ba, wx, bx, lam):
    xb = xc.astype(BF16)
    r = _sigmoid(jnp.dot(xb, wa, preferred_element_type=F32) + ba)
    i = _sigmoid(jnp.dot(xb, wx, preferred_element_type=F32) + bx)
    nl = -lam
    softplus = jnp.maximum(nl, 0.0) + jnp.log1p(jnp.exp(-jnp.abs(nl)))
    log_a = (-LRU_C) * r * softplus
    a = jnp.exp(log_a)
    u = jnp.sqrt(-jnp.tanh(log_a) * (a * a + 1.0)) * (i * xc)
    return a, u


def _rglru_prompt_kernel(xr_ref, gate_ref, cw_ref, cb_ref, wa_ref, ba_ref, wx_ref, bx_ref, lam_ref,
                         y_ref, hlast_ref, ext_ref, a_ref, u_ref, hc_ref):
    t = pl.program_id(1)
    tt = xr_ref.shape[1]

    @pl.when(t == 0)
    def _():
        ext_ref[0:SUBLANES, :] = jnp.zeros((SUBLANES, D_RNN), F32)
        hc_ref[...] = jnp.zeros(hc_ref.shape, F32)

    ext_ref[SUBLANES:SUBLANES + tt, :] = xr_ref[0]
    cw = cw_ref[...]
    xc = cb_ref[...]
    for w in range(CONV_WIDTH):
        lo = SUBLANES - (CONV_WIDTH - 1) + w
        xc = xc + cw[w:w + 1, :] * ext_ref[lo:lo + tt, :]
    ext_ref[0:SUBLANES, :] = ext_ref[tt:tt + SUBLANES, :]

    a, u = _lru_gates(xc, wa_ref[...], ba_ref[...], wx_ref[...], bx_ref[...], lam_ref[...])
    a_ref[...] = a
    u_ref[...] = u

    row = lax.broadcasted_iota(jnp.int32, (SUBLANES, D_RNN), 0)

    def group(g, hc):
        off = pl.multiple_of(g * SUBLANES, SUBLANES)
        a8 = a_ref[pl.ds(off, SUBLANES), :]
        u8 = u_ref[pl.ds(off, SUBLANES), :]
        for s in (1, 2, 4):
            a_sh = pltpu.roll(a8, s, 0)
            u_sh = pltpu.roll(u8, s, 0)
            m = row >= s
            u8 = jnp.where(m, u8 + a8 * u_sh, u8)
            a8 = jnp.where(m, a8 * a_sh, a8)
        h8 = u8 + a8 * hc
        u_ref[pl.ds(off, SUBLANES), :] = h8
        return h8[SUBLANES - 1:SUBLANES, :]

    hc = lax.fori_loop(0, tt // SUBLANES, group, hc_ref[0:1, :])
    hc_ref[...] = jnp.broadcast_to(hc, hc_ref.shape)
    hlast_ref[0] = hc
    y_ref[0] = u_ref[...] * _gelu(gate_ref[0])


def _rglru_prompt(xr, gate, cw, cb, wa, ba, wx, bx, lam):
    b, t, c = xr.shape
    tt = _tile(t, 512)
    seq = lambda i, j: (i, j, 0)
    vec = lambda i, j: (0, 0)
    return pl.pallas_call(
        _rglru_prompt_kernel,
        grid=(b, t // tt),
        in_specs=[pl.BlockSpec((1, tt, c), seq), pl.BlockSpec((1, tt, c), seq),
                  pl.BlockSpec((CONV_WIDTH, c), vec), pl.BlockSpec((1, c), vec),
                  pl.BlockSpec((c, c), vec), pl.BlockSpec((1, c), vec),
                  pl.BlockSpec((c, c), vec), pl.BlockSpec((1, c), vec), pl.BlockSpec((1, c), vec)],
        out_specs=[pl.BlockSpec((1, tt, c), seq), pl.BlockSpec((1, 1, c), lambda i, j: (i, 0, 0))],
        out_shape=[jax.ShapeDtypeStruct((b, t, c), F32), jax.ShapeDtypeStruct((b, 1, c), F32)],
        scratch_shapes=[pltpu.VMEM((tt + SUBLANES, c), F32), pltpu.VMEM((tt, c), F32),
                        pltpu.VMEM((tt, c), F32), pltpu.VMEM((SUBLANES, c), F32)],
        compiler_params=_params("parallel", "arbitrary"),
        name="rglru_prompt",
    )(xr, gate, cw, cb, wa, ba, wx, bx, lam)


def _rglru_sample_kernel(xr_ref, gate_ref, cp_ref, h0_ref, cw_ref, cb_ref, wa_ref, ba_ref, wx_ref, bx_ref,
                         lam_ref, y_ref, hlast_ref):
    t_steps = xr_ref.shape[0]
    rows = [cp_ref[i] for i in range(CONV_WIDTH - 1)] + [xr_ref[i] for i in range(t_steps)]
    cw = cw_ref[...]
    h = h0_ref[...]
    for t in range(t_steps):
        xc = cb_ref[...]
        for w in range(CONV_WIDTH):
            xc = xc + cw[w:w + 1, :] * rows[t + w]
        a, u = _lru_gates(xc, wa_ref[...], ba_ref[...], wx_ref[...], bx_ref[...], lam_ref[...])
        h = a * h + u
        y_ref[t] = h * _gelu(gate_ref[t])
    hlast_ref[...] = h


def _rglru_sample(xr_t, gate_t, cp_t, h0, cw, cb, wa, ba, wx, bx, lam):
    t, b, c = xr_t.shape
    return pl.pallas_call(
        _rglru_sample_kernel,
        out_shape=[jax.ShapeDtypeStruct((t, b, c), F32), jax.ShapeDtypeStruct((b, c), F32)],
        compiler_params=pltpu.CompilerParams(vmem_limit_bytes=VMEM_LIMIT),
        name="rglru_sample",
    )(xr_t, gate_t, cp_t, h0, cw, cb, wa, ba, wx, bx, lam)


ATTN_TQ = 256
NT_DIMS = (((1,), (1,)), ((), ()))


def _attn_prompt_kernel(q_ref, k_ref, v_ref, bias_ref, o_ref):
    i = pl.program_id(2)
    tq = q_ref.shape[1]
    w = bias_ref.shape[2]
    start = pl.multiple_of(jnp.maximum((i + 1) * tq - w, 0), LANES)
    kw = k_ref[0, pl.ds(start, w), :]
    vw = v_ref[0, pl.ds(start, w), :]
    bias = bias_ref[0]
    q = q_ref[0]
    even = lax.broadcasted_iota(jnp.int32, (tq, HEAD_PAIR), 1) < HEAD_DIM
    zero = jnp.zeros_like(q)

    def head(qh):
        s = lax.dot_general(qh, kw, NT_DIMS, preferred_element_type=F32) + bias
        m = jnp.max(s, axis=-1, keepdims=True)
        p = jnp.exp(s - m)
        den = jnp.sum(p, axis=-1, keepdims=True)
        return jnp.dot(p.astype(BF16), vw, preferred_element_type=F32) / den

    o_ref[0] = jnp.where(even, head(jnp.where(even, q, zero)), head(jnp.where(even, zero, q)))


def _attn_prompt(q, k, v):
    b, s, _ = q.shape
    tq = _tile(s, ATTN_TQ)
    w = min(s, MAX_WINDOW + tq)
    assert w % tq == 0 and w % LANES == 0
    n_off = w // tq
    ti = lax.broadcasted_iota(jnp.int32, (n_off, tq, w), 0)
    r = lax.broadcasted_iota(jnp.int32, (n_off, tq, w), 1)
    c = lax.broadcasted_iota(jnp.int32, (n_off, tq, w), 2)
    bias = _log_multiplicity(ti * tq + r - c)
    return pl.pallas_call(
        _attn_prompt_kernel,
        grid=(b, D_ATTN // HEAD_PAIR, s // tq),
        in_specs=[pl.BlockSpec((1, tq, HEAD_PAIR), lambda bi, hp, i: (bi, i, hp)),
                  pl.BlockSpec((1, s, HEAD_PAIR), lambda bi, hp, i: (bi, 0, hp)),
                  pl.BlockSpec((1, s, HEAD_PAIR), lambda bi, hp, i: (bi, 0, hp)),
                  pl.BlockSpec((1, tq, w), lambda bi, hp, i: (jnp.minimum(i, n_off - 1), 0, 0))],
        out_specs=pl.BlockSpec((1, tq, HEAD_PAIR), lambda bi, hp, i: (bi, i, hp)),
        out_shape=jax.ShapeDtypeStruct((b, s, D_ATTN), F32),
        compiler_params=_params("parallel", "parallel", "arbitrary"),
        name="attn_prompt",
    )(q, k, v, bias)


SAMPLE_ROWS = 16


def _attn_sample_kernel(t_new, q_ref, kt_ref, vt_ref, kn_ref, vn_ref, bias_ref, o_ref):
    bias = bias_ref[...]
    row = lax.broadcasted_iota(jnp.int32, (SAMPLE_ROWS, 1), 0)
    for h in range(N_HEADS):
        q = q_ref[0, h]
        s = jnp.dot(q, kt_ref[0, h].astype(BF16), preferred_element_type=F32) + bias
        qf = q.astype(F32)
        kn = kn_ref[0, h].astype(F32)
        vn = vn_ref[0, h].astype(F32)
        ln = []
        for jj in range(t_new):
            l = jnp.sum(qf * kn[jj:jj + 1, :], axis=-1, keepdims=True)
            for delta in range(t_new - jj):
                lb = float(np.log(max(int(_multiplicity(delta)), 1)))
                if lb != 0.0:
                    l = l + jnp.where(row - jj == delta, np.float32(lb), np.float32(0.0))
            ln.append(jnp.where(row >= jj, l, NEG))
        m = functools.reduce(jnp.maximum, ln, jnp.max(s, axis=-1, keepdims=True))
        p = jnp.exp(s - m)
        den = jnp.sum(p, axis=-1, keepdims=True)
        acc = lax.dot_general(p.astype(BF16), vt_ref[0, h].astype(BF16), NT_DIMS, preferred_element_type=F32)
        for jj in range(t_new):
            pn = jnp.exp(ln[jj] - m)
            den = den + pn
            acc = acc + pn * vn[jj:jj + 1, :]
        o_ref[0, h] = acc / den


def _attn_sample(q, k_new, v_new, cache_k, cache_v):
    b, t, _ = q.shape
    p = cache_k.shape[1]
    assert t <= SAMPLE_ROWS and all(_multiplicity(d) > 0 for d in range(t))
    heads = lambda a: jnp.pad(a.reshape(b, t, N_HEADS, HEAD_DIM).transpose(0, 2, 1, 3),
                              ((0, 0), (0, 0), (0, SAMPLE_ROWS - t), (0, 0)))
    lanes = lambda a: a.transpose(0, 2, 3, 1)
    jq = lax.broadcasted_iota(jnp.int32, (SAMPLE_ROWS, p), 0)
    kp = lax.broadcasted_iota(jnp.int32, (SAMPLE_ROWS, p), 1)
    bias = jnp.where(jq < t, _log_multiplicity(p + jq - kp), 0.0)
    new = pl.BlockSpec((1, N_HEADS, SAMPLE_ROWS, HEAD_DIM), lambda i: (i, 0, 0, 0))
    old = pl.BlockSpec((1, N_HEADS, HEAD_DIM, p), lambda i: (i, 0, 0, 0))
    out = pl.pallas_call(
        functools.partial(_attn_sample_kernel, t),
        grid=(b,),
        in_specs=[new, old, old, new, new, pl.BlockSpec((SAMPLE_ROWS, p), lambda i: (0, 0))],
        out_specs=new,
        out_shape=jax.ShapeDtypeStruct((b, N_HEADS, SAMPLE_ROWS, HEAD_DIM), F32),
        compiler_params=_params("parallel"),
        name="attn_sample",
    )(heads(q), lanes(cache_k), lanes(cache_v), heads(k_new), heads(v_new), bias)
    return out[:, :, :t].transpose(0, 2, 1, 3).reshape(b, t, D_ATTN)


def _merge_kernel(yr_ref, ya_ref, x_ref, gr_ref, ga_ref, wo_ref, gf_ref, x1_ref, xt_ref):
    a = _rms(yr_ref[...], gr_ref[...]).astype(BF16)
    b = _rms(ya_ref[...], ga_ref[...]).astype(BF16)
    merged = (jnp.dot(a, wo_ref[0:D_RNN, :], preferred_element_type=F32)
              + jnp.dot(b, wo_ref[D_RNN:D_RNN + D_ATTN, :], preferred_element_type=F32))
    x1 = x_ref[...] + merged
    x1_ref[...] = x1
    xt_ref[...] = _rms(x1, gf_ref[...]).T.astype(BF16)


def _merge(y_rnn, y_attn, x, g_rnn, g_attn, w_out_bf, g_ffn):
    n, d = x.shape
    tm = _tile(n, 512)
    row = lambda i: (i, 0)
    full = lambda i: (0, 0)
    return pl.pallas_call(
        _merge_kernel,
        grid=(n // tm,),
        in_specs=[pl.BlockSpec((tm, D_RNN), row), pl.BlockSpec((tm, D_ATTN), row), pl.BlockSpec((tm, d), row),
                  pl.BlockSpec((1, D_RNN), full), pl.BlockSpec((1, D_ATTN), full),
                  pl.BlockSpec((D_RNN + D_ATTN, d), full), pl.BlockSpec((1, d), full)],
        out_specs=[pl.BlockSpec((tm, d), row), pl.BlockSpec((d, tm), lambda i: (0, i))],
        out_shape=[jax.ShapeDtypeStruct((n, d), F32), jax.ShapeDtypeStruct((d, n), BF16)],
        compiler_params=_params("parallel"),
        name="merge",
    )(y_rnn, y_attn, x, g_rnn, g_attn, w_out_bf, g_ffn)


N_SORT = PEER_TOPK + 1
ROUTE_LANES = 4 * LANES


def _top_sorted(s, with_rank):
    tn = s.shape[1]
    row = lax.broadcasted_iota(jnp.int32, (SUBLANES, tn), 0)
    blocks = [jnp.full((SUBLANES, tn), NEG, F32) for _ in range(-(-N_SORT // SUBLANES))]
    rank = jnp.full(s.shape, float(PEER_TOPK), F32) if with_rank else None
    cur = s
    for k in range(N_SORT):
        mk = jnp.max(cur, axis=0, keepdims=True)
        blocks[k // SUBLANES] = jnp.where(row == k % SUBLANES, mk, blocks[k // SUBLANES])
        if k + 1 < N_SORT:
            hit = cur == mk
            if with_rank:
                rank = jnp.where(hit, float(k), rank)
            cur = jnp.where(hit, NEG, cur)
    return blocks, rank


def _candidates(a_blk, b_blk):
    a0, a1, a2 = a_blk
    b0, b1, b2 = b_blk
    tn = a0.shape[1]
    row = lax.broadcasted_iota(jnp.int32, (SUBLANES, tn), 0)
    out = [a0 + b0[0:1, :], a1 + b0[0:1, :], b1 + a0[0:1, :]]
    for l in range(1, SUBLANES):
        n_k = N_SORT // (l + 1)
        c = a0 + b0[l:l + 1, :]
        out.append(c if n_k >= SUBLANES else jnp.where(row < n_k, c, NEG))
    out.append(jnp.where(row == 0, a2[0:1, :] + b0[0:1, :], jnp.where(row == 1, b2[0:1, :] + a0[0:1, :], NEG)))
    return out


def _route_kernel(xt_ref, wq_ref, keys_ref, n_ref, e1_ref, rank_ref, e2_ref, q_ref):
    q_ref[...] = jnp.dot(wq_ref[...], xt_ref[...], preferred_element_type=F32).astype(BF16)

    def head(h, ls):
        r1 = pl.multiple_of(h * (2 * PEER_HALF), 2 * PEER_HALF)
        r2 = pl.multiple_of(r1 + PEER_HALF, PEER_HALF)
        s1 = jnp.dot(keys_ref[2 * h], q_ref[pl.ds(r1, PEER_HALF), ls], preferred_element_type=F32)
        s2 = jnp.dot(keys_ref[2 * h + 1], q_ref[pl.ds(r2, PEER_HALF), ls], preferred_element_type=F32)
        a_blk, _ = _top_sorted(s1, False)
        b_blk, rank2 = _top_sorted(s2, True)
        cands = _candidates(a_blk, b_blk)
        cur = list(cands)
        kth = None
        for k in range(N_SORT):
            mx = functools.reduce(jnp.maximum, cur)
            mx = jnp.max(mx, axis=0, keepdims=True)
            if k == PEER_TOPK - 1:
                kth = mx
            if k + 1 < N_SORT:
                cur = [jnp.where(c == mx, NEG, c) for c in cur]
        tau = 0.5 * (kth + mx)
        m1 = a_blk[0][0:1, :]
        m2 = b_blk[0][0:1, :]
        z = functools.reduce(
            lambda x, y: x + y,
            [jnp.where(c >= tau, jnp.exp(c - (m1 + m2)), 0.0) for c in cands])
        z = jnp.sum(z, axis=0, keepdims=True)
        thr = tau - s1
        n = jnp.zeros(s1.shape, F32)
        for l in range(PEER_TOPK):
            b_l = b_blk[l // SUBLANES][l % SUBLANES:l % SUBLANES + 1, :]
            n = n + jnp.where(b_l >= thr, 1.0, 0.0)
        o = pl.multiple_of(h * N_KEYS, N_KEYS)
        n_ref[pl.ds(o, N_KEYS), ls] = n
        e1_ref[pl.ds(o, N_KEYS), ls] = jnp.exp(s1 - m1) / z
        rank_ref[pl.ds(o, N_KEYS), ls] = rank2
        e2_ref[pl.ds(o, N_KEYS), ls] = jnp.exp(s2 - m2)

    tn = xt_ref.shape[1]
    lb = min(tn, ROUTE_LANES)
    for b in range(tn // lb):
        ls = slice(b * lb, (b + 1) * lb)

        def body(h, carry, ls=ls):
            head(h, ls)
            return carry

        lax.fori_loop(0, PEER_HEADS, body, 0)


def _route(xt, wq_t, keys):
    d, n = xt.shape
    tn = _tile(n, 512)
    rows = PEER_HEADS * N_KEYS
    col = lambda i: (0, i)
    o = pl.BlockSpec((rows, tn), col)
    return pl.pallas_call(
        _route_kernel,
        grid=(n // tn,),
        in_specs=[pl.BlockSpec((d, tn), col), pl.BlockSpec(wq_t.shape, lambda i: (0, 0)),
                  pl.BlockSpec(keys.shape, lambda i: (0, 0, 0))],
        out_specs=[o] * 4,
        out_shape=[jax.ShapeDtypeStruct((rows, n), F32)] * 4,
        scratch_shapes=[pltpu.VMEM((wq_t.shape[0], tn), BF16)],
        compiler_params=_params("parallel"),
        name="peer_route",
    )(xt, wq_t, keys)


EXPERT_CHUNK = 1024
EXPERT_GROUP = 2


def _gelu_sigmoid(x):
    c = float(np.sqrt(2.0 / np.pi))
    z2 = x * ((-2.0 * c) + (-2.0 * c * 0.044715) * (x * x))
    return x / (1.0 + jnp.exp(z2))


def _peer_kernel(xt_ref, u_ref, vt_ref, n_ref, e1_ref, rank_ref, e2_ref, x1_ref, gfin_ref, y_ref,
                 acc_ref, w_ref, act_ref, rank_s, e2_s):
    c = pl.program_id(1)
    ec = u_ref.shape[0]
    tn = xt_ref.shape[1]
    n_i = ec // N_KEYS

    @pl.when(c == 0)
    def _():
        acc_ref[...] = jnp.zeros(acc_ref.shape, F32)
        for h in range(PEER_HEADS):
            hs = slice(h * N_KEYS, (h + 1) * N_KEYS)
            rank_s[hs, :] = rank_ref[hs, :].astype(BF16)
            e2_s[hs, :] = e2_ref[hs, :].astype(BF16)

    act_ref[...] = jnp.dot(u_ref[...], xt_ref[...], preferred_element_type=F32)
    for b in range(tn // LANES):
        ls = slice(b * LANES, (b + 1) * LANES)
        for i0 in range(0, n_i, EXPERT_GROUP):
            gs = [jnp.zeros((N_KEYS, LANES), BF16) for _ in range(EXPERT_GROUP)]
            for h in range(PEER_HEADS):
                hs = slice(h * N_KEYS, (h + 1) * N_KEYS)
                rank = rank_s[hs, ls]
                e2 = e2_s[hs, ls]
                zero = jnp.zeros_like(e2)
                for ii in range(EXPERT_GROUP):
                    r = h * N_KEYS + c * n_i + i0 + ii
                    n_row = n_ref[pl.ds(r, 1), :][:, ls].astype(BF16)
                    e1_row = e1_ref[pl.ds(r, 1), :][:, ls].astype(BF16)
                    gs[ii] = gs[ii] + jnp.where(rank < n_row, e2, zero) * e1_row
            for ii in range(EXPERT_GROUP):
                rs = slice((i0 + ii) * N_KEYS, (i0 + ii + 1) * N_KEYS)
                w_ref[rs, ls] = gs[ii] * _gelu_sigmoid(act_ref[rs, ls]).astype(BF16)
    acc_ref[...] += jnp.dot(vt_ref[...], w_ref[...], preferred_element_type=F32)

    @pl.when(c == pl.num_programs(1) - 1)
    def _():
        x2 = x1_ref[...] + acc_ref[...].T
        y_ref[...] = _rms(x2, gfin_ref[...])


def _peer(xt, u_bf, vt_bf, n, e1, rank, e2, x1, g_final):
    d, n_tok = xt.shape
    n_exp = u_bf.shape[0]
    tn = _tile(n_tok, 512)
    ec = EXPERT_CHUNK
    rows = PEER_HEADS * N_KEYS
    tok = lambda i, c: (0, i)
    sc = pl.BlockSpec((rows, tn), tok)
    return pl.pallas_call(
        _peer_kernel,
        grid=(n_tok // tn, n_exp // ec),
        in_specs=[pl.BlockSpec((d, tn), tok),
                  pl.BlockSpec((ec, d), lambda i, c: (c, 0)),
                  pl.BlockSpec((d, ec), lambda i, c: (0, c)),
                  sc, sc, sc, sc,
                  pl.BlockSpec((tn, d), lambda i, c: (i, 0)),
                  pl.BlockSpec((1, d), lambda i, c: (0, 0))],
        out_specs=pl.BlockSpec((tn, d), lambda i, c: (i, 0)),
        out_shape=jax.ShapeDtypeStruct((n_tok, d), F32),
        scratch_shapes=[pltpu.VMEM((d, tn), F32), pltpu.VMEM((ec, tn), BF16), pltpu.VMEM((ec, tn), F32),
                        pltpu.VMEM((rows, tn), BF16), pltpu.VMEM((rows, tn), BF16)],
        compiler_params=_params("parallel", "arbitrary"),
        name="peer_experts",
    )(xt, u_bf, vt_bf, n, e1, rank, e2, x1, g_final)


def _block_diag(w):
    return jax.scipy.linalg.block_diag(*[w[i] for i in range(w.shape[0])])


def kernel(x_prompt, x_sample, cache_k, cache_v, state_conv, state_h, g_mix, w_in, conv_w, conv_b, w_a, b_a, w_x,
           b_x, lru_lambda, g_rnn_out, g_attn_out, w_out, g_ffn, w_query, sub_keys, expert_u, expert_v, g_final):
    depth = g_mix.shape[0]
    assert depth == 1
    bp, sp, d = x_prompt.shape
    bs, ts, _ = x_sample.shape
    assert ts >= CONV_WIDTH - 1
    l = 0
    w_in_bf = w_in[l].astype(BF16)
    w_out_bf = w_out[l].astype(BF16)
    wa = _block_diag(w_a[l]).astype(BF16)
    wx = _block_diag(w_x[l]).astype(BF16)
    row = lambda v: v.reshape(1, -1)
    cw, cb, ba, bx, lam = conv_w[l], row(conv_b[l]), row(b_a[l]), row(b_x[l]), row(lru_lambda[l])
    wq_t = w_query[l].T.astype(BF16)
    keys = sub_keys[l].reshape(2 * PEER_HEADS, N_KEYS, PEER_HALF).astype(BF16)
    u_bf = expert_u[l].astype(BF16)
    vt_bf = expert_v[l].T.astype(BF16)
    gfin = row(g_final)

    def ffn(y_rnn, y_attn, x):
        x1, xt = _merge(y_rnn, y_attn, x, row(g_rnn_out[l]), row(g_attn_out[l]), w_out_bf, row(g_ffn[l]))
        n, e1, rank, e2 = _route(xt, wq_t, keys)
        return _peer(xt, u_bf, vt_bf, n, e1, rank, e2, x1, gfin)

    xp = x_prompt.reshape(bp * sp, d)
    xr, gate, q, k, v, kf, vf = _inproj(xp, g_mix[l], w_in_bf)
    seq = lambda a: a.reshape(bp, sp, -1)
    y_rnn, h_p = _rglru_prompt(seq(xr), seq(gate), cw, cb, wa, ba, wx, bx, lam)
    y_attn = _attn_prompt(seq(q), seq(k), seq(v))
    y_p = ffn(y_rnn.reshape(bp * sp, -1), y_attn.reshape(bp * sp, -1), xp).reshape(bp, sp, d)
    keep = min(MAX_WINDOW, sp)
    k_p = seq(kf)[:, sp - keep:].reshape(1, bp, keep, N_HEADS, HEAD_DIM)
    v_p = seq(vf)[:, sp - keep:].reshape(1, bp, keep, N_HEADS, HEAD_DIM)
    conv_p = seq(xr)[:, sp - (CONV_WIDTH - 1):][None]
    h_p = h_p.reshape(1, bp, D_RNN)

    xs = x_sample.reshape(bs * ts, d)
    xr, gate, q, k, v, kf, vf = _inproj(xs, g_mix[l], w_in_bf)
    seq = lambda a: a.reshape(bs, ts, -1)
    tmaj = lambda a: jnp.swapaxes(seq(a), 0, 1)
    y_t, h_s = _rglru_sample(tmaj(xr), tmaj(gate), jnp.swapaxes(state_conv[l], 0, 1), state_h[l],
                             cw, cb, wa, ba, wx, bx, lam)
    y_rnn = jnp.swapaxes(y_t, 0, 1)
    y_attn = _attn_sample(seq(q), seq(k), seq(v), cache_k[l], cache_v[l])
    y_s = ffn(y_rnn.reshape(bs * ts, -1), y_attn.reshape(bs * ts, -1), xs).reshape(bs, ts, d)
    k_s = seq(kf).reshape(1, bs, ts, N_HEADS, HEAD_DIM)
    v_s = seq(vf).reshape(1, bs, ts, N_HEADS, HEAD_DIM)
    conv_s = seq(xr)[:, ts - (CONV_WIDTH - 1):][None]
    h_s = h_s.reshape(1, bs, D_RNN)

    return (y_p, y_s, k_p, v_p, k_s, v_s, conv_p, conv_s, h_p, h_s)
```

```python
import functools

import numpy as np
import jax
import jax.numpy as jnp
from jax import lax
from jax.experimental import pallas as pl
from jax.experimental.pallas import tpu as pltpu

F32 = jnp.float32
BF16 = jnp.bfloat16

EPS = 1e-6
D_RNN = 512
D_ATTN = 512
N_HEADS = 8
HEAD_DIM = 64
CONV_WIDTH = 4
LRU_C = 8.0
DILATED_PATTERNS = ((128, 1), (512, 4), (2048, 16))
MAX_WINDOW = 2048
N_KEYS = 128
PEER_HEADS = 8
PEER_HALF = 128
PEER_TOPK = 16

LANES = 128
SUBLANES = 8
BF16_ROWS = 2 * SUBLANES
HEAD_PAIR = LANES
NEG = -1e30
VMEM_LIMIT = 48 * 1024 * 1024


def _params(*sem):
    return pltpu.CompilerParams(dimension_semantics=sem, vmem_limit_bytes=VMEM_LIMIT)


def _tile(n, pref):
    return pref if n % pref == 0 else n


def _rms(x, g):
    return x * lax.rsqrt(jnp.mean(x * x, axis=-1, keepdims=True) + EPS) * g


def _gelu(x):
    c = np.float32(np.sqrt(2.0 / np.pi))
    return x * (0.5 * (1.0 + jnp.tanh(c * (x + 0.044715 * (x * x * x)))))


def _sigmoid(x):
    return 1.0 / (1.0 + jnp.exp(-x))


def _multiplicity(delta):
    delta = np.asarray(delta)
    cnt = np.zeros(delta.shape, np.int32)
    for window, dil in DILATED_PATTERNS:
        cnt += ((delta >= 0) & (delta <= window) & (delta % dil == 0)).astype(np.int32)
    return cnt


def _log_multiplicity(delta, same=None):
    cnt = jnp.zeros(delta.shape, F32)
    for window, dil in DILATED_PATTERNS:
        cnt = cnt + ((delta >= 0) & (delta <= window) & (delta % dil == 0)).astype(F32)
    ok = cnt > 0
    if same is not None:
        ok = ok & same
    return jnp.where(ok, jnp.log(jnp.maximum(cnt, 1.0)), NEG)


def _inproj_kernel(x_ref, g_ref, w_ref, xr_ref, gate_ref, q_ref, k_ref, v_ref, kf_ref, vf_ref):
    xn = _rms(x_ref[...], g_ref[...])
    z = jnp.dot(xn.astype(BF16), w_ref[...], preferred_element_type=F32)
    xr_ref[...] = z[:, 0:D_RNN]
    gate_ref[...] = z[:, D_RNN:2 * D_RNN]
    o = 2 * D_RNN
    q_ref[...] = (z[:, o:o + D_ATTN] * (HEAD_DIM ** -0.5)).astype(BF16)
    k = z[:, o + D_ATTN:o + 2 * D_ATTN]
    v = z[:, o + 2 * D_ATTN:o + 3 * D_ATTN]
    k_ref[...] = k.astype(BF16)
    v_ref[...] = v.astype(BF16)
    kf_ref[...] = k
    vf_ref[...] = v


def _inproj(x, g_mix, w_in_bf):
    n, d = x.shape
    d_in = w_in_bf.shape[1]
    tm = _tile(n, 512)
    row = lambda i: (i, 0)
    full = lambda i: (0, 0)
    o512 = pl.BlockSpec((tm, 512), row)
    return pl.pallas_call(
        _inproj_kernel,
        grid=(n // tm,),
        in_specs=[pl.BlockSpec((tm, d), row), pl.BlockSpec((1, d), full), pl.BlockSpec((d, d_in), full)],
        out_specs=[o512] * 7,
        out_shape=[jax.ShapeDtypeStruct((n, 512), F32), jax.ShapeDtypeStruct((n, 512), F32),
                   jax.ShapeDtypeStruct((n, 512), BF16), jax.ShapeDtypeStruct((n, 512), BF16),
                   jax.ShapeDtypeStruct((n, 512), BF16), jax.ShapeDtypeStruct((n, 512), F32),
                   jax.ShapeDtypeStruct((n, 512), F32)],
        compiler_params=_params("parallel"),
        name="inproj",
    )(x, g_mix.reshape(1, d), w_in_bf)


def _lru_gates(xc, wa, ba, wx, bx, lam):
    xb = xc.astype(BF16)
    r = _sigmoid(jnp.dot(xb, wa, preferred_element_type=F32) + ba)
    i = _sigmoid(jnp.dot(xb, wx, preferred_element_type=F32) + bx)
    nl = -lam
    softplus = jnp.maximum(nl, 0.0) + jnp.log1p(jnp.exp(-jnp.abs(nl)))
    log_a = (-LRU_C) * r * softplus
    a = jnp.exp(log_a)
    u = jnp.sqrt(-jnp.tanh(log_a) * (a * a + 1.0)) * (i * xc)
    return a, u


def _rglru_prompt_kernel(xr_ref, gate_ref, cw_ref, cb_ref, wa_ref, ba_ref, wx_ref, bx_ref, lam_ref,
                         y_ref, hlast_ref, ext_ref, a_ref, u_ref, hc_ref):
    t = pl.program_id(1)
    tt = xr_ref.shape[1]

    @pl.when(t == 0)
    def _():
        ext_ref[0:SUBLANES, :] = jnp.zeros((SUBLANES, D_RNN), F32)
        hc_ref[...] = jnp.zeros(hc_ref.shape, F32)

    ext_ref[SUBLANES:SUBLANES + tt, :] = xr_ref[0]
    cw = cw_ref[...]
    xc = cb_ref[...]
    for w in range(CONV_WIDTH):
        lo = SUBLANES - (CONV_WIDTH - 1) + w
        xc = xc + cw[w:w + 1, :] * ext_ref[lo:lo + tt, :]
    ext_ref[0:SUBLANES, :] = ext_ref[tt:tt + SUBLANES, :]

    a, u = _lru_gates(xc, wa_ref[...], ba_ref[...], wx_ref[...], bx_ref[...], lam_ref[...])
    a_ref[...] = a
    u_ref[...] = u

    row = lax.broadcasted_iota(jnp.int32, (SUBLANES, D_RNN), 0)

    def group(g, hc):
        off = pl.multiple_of(g * SUBLANES, SUBLANES)
        a8 = a_ref[pl.ds(off, SUBLANES), :]
        u8 = u_ref[pl.ds(off, SUBLANES), :]
        for s in (1, 2, 4):
            a_sh = pltpu.roll(a8, s, 0)
            u_sh = pltpu.roll(u8, s, 0)
            m = row >= s
            u8 = jnp.where(m, u8 + a8 * u_sh, u8)
            a8 = jnp.where(m, a8 * a_sh, a8)
        h8 = u8 + a8 * hc
        u_ref[pl.ds(off, SUBLANES), :] = h8
        return h8[SUBLANES - 1:SUBLANES, :]

    hc = lax.fori_loop(0, tt // SUBLANES, group, hc_ref[0:1, :])
    hc_ref[...] = jnp.broadcast_to(hc, hc_ref.shape)
    hlast_ref[0] = hc
    y_ref[0] = u_ref[...] * _gelu(gate_ref[0])


def _rglru_prompt(xr, gate, cw, cb, wa, ba, wx, bx, lam):
    b, t, c = xr.shape
    tt = _tile(t, 512)
    seq = lambda i, j: (i, j, 0)
    vec = lambda i, j: (0, 0)
    return pl.pallas_call(
        _rglru_prompt_kernel,
        grid=(b, t // tt),
        in_specs=[pl.BlockSpec((1, tt, c), seq), pl.BlockSpec((1, tt, c), seq),
                  pl.BlockSpec((CONV_WIDTH, c), vec), pl.BlockSpec((1, c), vec),
                  pl.BlockSpec((c, c), vec), pl.BlockSpec((1, c), vec),
                  pl.BlockSpec((c, c), vec), pl.BlockSpec((1, c), vec), pl.BlockSpec((1, c), vec)],
        out_specs=[pl.BlockSpec((1, tt, c), seq), pl.BlockSpec((1, 1, c), lambda i, j: (i, 0, 0))],
        out_shape=[jax.ShapeDtypeStruct((b, t, c), F32), jax.ShapeDtypeStruct((b, 1, c), F32)],
        scratch_shapes=[pltpu.VMEM((tt + SUBLANES, c), F32), pltpu.VMEM((tt, c), F32),
                        pltpu.VMEM((tt, c), F32), pltpu.VMEM((SUBLANES, c), F32)],
        compiler_params=_params("parallel", "arbitrary"),
        name="rglru_prompt",
    )(xr, gate, cw, cb, wa, ba, wx, bx, lam)


def _rglru_sample_kernel(xr_ref, gate_ref, cp_ref, h0_ref, cw_ref, cb_ref, wa_ref, ba_ref, wx_ref, bx_ref,
                         lam_ref, y_ref, hlast_ref):
    t_steps = xr_ref.shape[0]
    rows = [cp_ref[i] for i in range(CONV_WIDTH - 1)] + [xr_ref[i] for i in range(t_steps)]
    cw = cw_ref[...]
    h = h0_ref[...]
    for t in range(t_steps):
        xc = cb_ref[...]
        for w in range(CONV_WIDTH):
            xc = xc + cw[w:w + 1, :] * rows[t + w]
        a, u = _lru_gates(xc, wa_ref[...], ba_ref[...], wx_ref[...], bx_ref[...], lam_ref[...])
        h = a * h + u
        y_ref[t] = h * _gelu(gate_ref[t])
    hlast_ref[...] = h


def _rglru_sample(xr_t, gate_t, cp_t, h0, cw, cb, wa, ba, wx, bx, lam):
    t, b, c = xr_t.shape
    return pl.pallas_call(
        _rglru_sample_kernel,
        out_shape=[jax.ShapeDtypeStruct((t, b, c), F32), jax.ShapeDtypeStruct((b, c), F32)],
        compiler_params=pltpu.CompilerParams(vmem_limit_bytes=VMEM_LIMIT),
        name="rglru_sample",
    )(xr_t, gate_t, cp_t, h0, cw, cb, wa, ba, wx, bx, lam)


ATTN_TQ = 256
NT_DIMS = (((1,), (1,)), ((), ()))


def _attn_prompt_kernel(q_ref, k_ref, v_ref, bias_ref, o_ref):
    i = pl.program_id(2)
    tq = q_ref.shape[1]
    w = bias_ref.shape[2]
    start = pl.multiple_of(jnp.maximum((i + 1) * tq - w, 0), LANES)
    kw = k_ref[0, pl.ds(start, w), :]
    vw = v_ref[0, pl.ds(start, w), :]
    bias = bias_ref[0]
    q = q_ref[0]
    even = lax.broadcasted_iota(jnp.int32, (tq, HEAD_PAIR), 1) < HEAD_DIM
    zero = jnp.zeros_like(q)

    def head(qh):
        s = lax.dot_general(qh, kw, NT_DIMS, preferred_element_type=F32) + bias
        m = jnp.max(s, axis=-1, keepdims=True)
        p = jnp.exp(s - m)
        den = jnp.sum(p, axis=-1, keepdims=True)
        return jnp.dot(p.astype(BF16), vw, preferred_element_type=F32) / den

    o_ref[0] = jnp.where(even, head(jnp.where(even, q, zero)), head(jnp.where(even, zero, q)))


def _attn_prompt(q, k, v):
    b, s, _ = q.shape
    tq = _tile(s, ATTN_TQ)
    w = min(s, MAX_WINDOW + tq)
    assert w % tq == 0 and w % LANES == 0
    n_off = w // tq
    ti = lax.broadcasted_iota(jnp.int32, (n_off, tq, w), 0)
    r = lax.broadcasted_iota(jnp.int32, (n_off, tq, w), 1)
    c = lax.broadcasted_iota(jnp.int32, (n_off, tq, w), 2)
    bias = _log_multiplicity(ti * tq + r - c)
    return pl.pallas_call(
        _attn_prompt_kernel,
        grid=(b, D_ATTN // HEAD_PAIR, s // tq),
        in_specs=[pl.BlockSpec((1, tq, HEAD_PAIR), lambda bi, hp, i: (bi, i, hp)),
                  pl.BlockSpec((1, s, HEAD_PAIR), lambda bi, hp, i: (bi, 0, hp)),
                  pl.BlockSpec((1, s, HEAD_PAIR), lambda bi, hp, i: (bi, 0, hp)),
                  pl.BlockSpec((1, tq, w), lambda bi, hp, i: (jnp.minimum(i, n_off - 1), 0, 0))],
        out_specs=pl.BlockSpec((1, tq, HEAD_PAIR), lambda bi, hp, i: (bi, i, hp)),
        out_shape=jax.ShapeDtypeStruct((b, s, D_ATTN), F32),
        compiler_params=_params("parallel", "parallel", "arbitrary"),
        name="attn_prompt",
    )(q, k, v, bias)


SAMPLE_ROWS = 16


def _attn_sample_kernel(t_new, q_ref, kt_ref, vt_ref, kn_ref, vn_ref, bias_ref, o_ref):
    bias = bias_ref[...]
    row = lax.broadcasted_iota(jnp.int32, (SAMPLE_ROWS, 1), 0)
    for h in range(N_HEADS):
        q = q_ref[0, h]
        s = jnp.dot(q, kt_ref[0, h].astype(BF16), preferred_element_type=F32) + bias
        qf = q.astype(F32)
        kn = kn_ref[0, h].astype(F32)
        vn = vn_ref[0, h].astype(F32)
        ln = []
        for jj in range(t_new):
            l = jnp.sum(qf * kn[jj:jj + 1, :], axis=-1, keepdims=True)
            for delta in range(t_new - jj):
                lb = float(np.log(max(int(_multiplicity(delta)), 1)))
                if lb != 0.0:
                    l = l + jnp.where(row - jj == delta, np.float32(lb), np.float32(0.0))
            ln.append(jnp.where(row >= jj, l, NEG))
        m = functools.reduce(jnp.maximum, ln, jnp.max(s, axis=-1, keepdims=True))
        p = jnp.exp(s - m)
        den = jnp.sum(p, axis=-1, keepdims=True)
        acc = lax.dot_general(p.astype(BF16), vt_ref[0, h].astype(BF16), NT_DIMS, preferred_element_type=F32)
        for jj in range(t_new):
            pn = jnp.exp(ln[jj] - m)
            den = den + pn
            acc = acc + pn * vn[jj:jj + 1, :]
        o_ref[0, h] = acc / den


def _attn_sample(q, k_new, v_new, cache_k, cache_v):
    b, t, _ = q.shape
    p = cache_k.shape[1]
    assert t <= SAMPLE_ROWS and all(_multiplicity(d) > 0 for d in range(t))
    heads = lambda a: jnp.pad(a.reshape(b, t, N_HEADS, HEAD_DIM).transpose(0, 2, 1, 3),
                              ((0, 0), (0, 0), (0, SAMPLE_ROWS - t), (0, 0)))
    lanes = lambda a: a.transpose(0, 2, 3, 1)
    jq = lax.broadcasted_iota(jnp.int32, (SAMPLE_ROWS, p), 0)
    kp = lax.broadcasted_iota(jnp.int32, (SAMPLE_ROWS, p), 1)
    bias = jnp.where(jq < t, _log_multiplicity(p + jq - kp), 0.0)
    new = pl.BlockSpec((1, N_HEADS, SAMPLE_ROWS, HEAD_DIM), lambda i: (i, 0, 0, 0))
    old = pl.BlockSpec((1, N_HEADS, HEAD_DIM, p), lambda i: (i, 0, 0, 0))
    out = pl.pallas_call(
        functools.partial(_attn_sample_kernel, t),
        grid=(b,),
        in_specs=[new, old, old, new, new, pl.BlockSpec((SAMPLE_ROWS, p), lambda i: (0, 0))],
        out_specs=new,
        out_shape=jax.ShapeDtypeStruct((b, N_HEADS, SAMPLE_ROWS, HEAD_DIM), F32),
        compiler_params=_params("parallel"),
        name="attn_sample",
    )(heads(q), lanes(cache_k), lanes(cache_v), heads(k_new), heads(v_new), bias)
    return out[:, :, :t].transpose(0, 2, 1, 3).reshape(b, t, D_ATTN)


def _merge_kernel(yr_ref, ya_ref, x_ref, gr_ref, ga_ref, wo_ref, gf_ref, x1_ref, xt_ref):
    a = _rms(yr_ref[...], gr_ref[...]).astype(BF16)
    b = _rms(ya_ref[...], ga_ref[...]).astype(BF16)
    merged = (jnp.dot(a, wo_ref[0:D_RNN, :], preferred_element_type=F32)
              + jnp.dot(b, wo_ref[D_RNN:D_RNN + D_ATTN, :], preferred_element_type=F32))
    x1 = x_ref[...] + merged
    x1_ref[...] = x1
    xt_ref[...] = _rms(x1, gf_ref[...]).T.astype(BF16)


def _merge(y_rnn, y_attn, x, g_rnn, g_attn, w_out_bf, g_ffn):
    n, d = x.shape
    tm = _tile(n, 512)
    row = lambda i: (i, 0)
    full = lambda i: (0, 0)
    return pl.pallas_call(
        _merge_kernel,
        grid=(n // tm,),
        in_specs=[pl.BlockSpec((tm, D_RNN), row), pl.BlockSpec((tm, D_ATTN), row), pl.BlockSpec((tm, d), row),
                  pl.BlockSpec((1, D_RNN), full), pl.BlockSpec((1, D_ATTN), full),
                  pl.BlockSpec((D_RNN + D_ATTN, d), full), pl.BlockSpec((1, d), full)],
        out_specs=[pl.BlockSpec((tm, d), row), pl.BlockSpec((d, tm), lambda i: (0, i))],
        out_shape=[jax.ShapeDtypeStruct((n, d), F32), jax.ShapeDtypeStruct((d, n), BF16)],
        compiler_params=_params("parallel"),
        name="merge",
    )(y_rnn, y_attn, x, g_rnn, g_attn, w_out_bf, g_ffn)


N_SORT = PEER_TOPK + 1
ROUTE_LANES = 4 * LANES


def _top_sorted(s, with_rank):
    tn = s.shape[1]
    row = lax.broadcasted_iota(jnp.int32, (SUBLANES, tn), 0)
    blocks = [jnp.full((SUBLANES, tn), NEG, F32) for _ in range(-(-N_SORT // SUBLANES))]
    rank = jnp.full(s.shape, float(PEER_TOPK), F32) if with_rank else None
    cur = s
    for k in range(N_SORT):
        mk = jnp.max(cur, axis=0, keepdims=True)
        blocks[k // SUBLANES] = jnp.where(row == k % SUBLANES, mk, blocks[k // SUBLANES])
        if k + 1 < N_SORT:
            hit = cur == mk
            if with_rank:
                rank = jnp.where(hit, float(k), rank)
            cur = jnp.where(hit, NEG, cur)
    return blocks, rank


def _candidates(a_blk, b_blk):
    a0, a1, a2 = a_blk
    b0, b1, b2 = b_blk
    tn = a0.shape[1]
    row = lax.broadcasted_iota(jnp.int32, (SUBLANES, tn), 0)
    out = [a0 + b0[0:1, :], a1 + b0[0:1, :], b1 + a0[0:1, :]]
    for l in range(1, SUBLANES):
        n_k = N_SORT // (l + 1)
        c = a0 + b0[l:l + 1, :]
        out.append(c if n_k >= SUBLANES else jnp.where(row < n_k, c, NEG))
    out.append(jnp.where(row == 0, a2[0:1, :] + b0[0:1, :], jnp.where(row == 1, b2[0:1, :] + a0[0:1, :], NEG)))
    return out


def _route_kernel(xt_ref, wq_ref, keys_ref, n_ref, e1_ref, rank_ref, e2_ref, q_ref):
    q_ref[...] = jnp.dot(wq_ref[...], xt_ref[...], preferred_element_type=F32).astype(BF16)

    def head(h, ls):
        r1 = pl.multiple_of(h * (2 * PEER_HALF), 2 * PEER_HALF)
        r2 = pl.multiple_of(r1 + PEER_HALF, PEER_HALF)
        s1 = jnp.dot(keys_ref[2 * h], q_ref[pl.ds(r1, PEER_HALF), ls], preferred_element_type=F32)
        s2 = jnp.dot(keys_ref[2 * h + 1], q_ref[pl.ds(r2, PEER_HALF), ls], preferred_element_type=F32)
        a_blk, _ = _top_sorted(s1, False)
        b_blk, rank2 = _top_sorted(s2, True)
        cands = _candidates(a_blk, b_blk)
        cur = list(cands)
        kth = None
        for k in range(N_SORT):
            mx = functools.reduce(jnp.maximum, cur)
            mx = jnp.max(mx, axis=0, keepdims=True)
            if k == PEER_TOPK - 1:
                kth = mx
            if k + 1 < N_SORT:
                cur = [jnp.where(c == mx, NEG, c) for c in cur]
        tau = 0.5 * (kth + mx)
        m1 = a_blk[0][0:1, :]
        m2 = b_blk[0][0:1, :]
        z = functools.reduce(
            lambda x, y: x + y,
            [jnp.where(c >= tau, jnp.exp(c - (m1 + m2)), 0.0) for c in cands])
        z = jnp.sum(z, axis=0, keepdims=True)
        thr = tau - s1
        n = jnp.zeros(s1.shape, F32)
        for l in range(PEER_TOPK):
            b_l = b_blk[l // SUBLANES][l % SUBLANES:l % SUBLANES + 1, :]
            n = n + jnp.where(b_l >= thr, 1.0, 0.0)
        o = pl.multiple_of(h * N_KEYS, N_KEYS)
        n_ref[pl.ds(o, N_KEYS), ls] = n
        e1_ref[pl.ds(o, N_KEYS), ls] = jnp.exp(s1 - m1) / z
        rank_ref[pl.ds(o, N_KEYS), ls] = rank2
        e2_ref[pl.ds(o, N_KEYS), ls] = jnp.exp(s2 - m2)

    tn = xt_ref.shape[1]
    lb = min(tn, ROUTE_LANES)
    for b in range(tn // lb):
        ls = slice(b * lb, (b + 1) * lb)

        def body(h, carry, ls=ls):
            head(h, ls)
            return carry

        lax.fori_loop(0, PEER_HEADS, body, 0)


def _route(xt, wq_t, keys):
    d, n = xt.shape
    tn = _tile(n, 512)
    rows = PEER_HEADS * N_KEYS
    col = lambda i: (0, i)
    o = pl.BlockSpec((rows, tn), col)
    return pl.pallas_call(
        _route_kernel,
        grid=(n // tn,),
        in_specs=[pl.BlockSpec((d, tn), col), pl.BlockSpec(wq_t.shape, lambda i: (0, 0)),
                  pl.BlockSpec(keys.shape, lambda i: (0, 0, 0))],
        out_specs=[o] * 4,
        out_shape=[jax.ShapeDtypeStruct((rows, n), F32)] * 4,
        scratch_shapes=[pltpu.VMEM((wq_t.shape[0], tn), BF16)],
        compiler_params=_params("parallel"),
        name="peer_route",
    )(xt, wq_t, keys)


EXPERT_CHUNK = 1024


def _gelu_sigmoid(x):
    c = float(np.sqrt(2.0 / np.pi))
    z2 = x * ((-2.0 * c) + (-2.0 * c * 0.044715) * (x * x))
    return x / (1.0 + jnp.exp(z2))


def _peer_kernel(xt_ref, u_ref, vt_ref, n_ref, e1_ref, rank_ref, e2_ref, x1_ref, gfin_ref, y_ref,
                 acc_ref, w_ref, act_ref, rank_s, e2_s):
    c = pl.program_id(1)
    ec = u_ref.shape[0]
    tn = xt_ref.shape[1]
    n_i = ec // N_KEYS

    @pl.when(c == 0)
    def _():
        acc_ref[...] = jnp.zeros(acc_ref.shape, F32)
        for h in range(PEER_HEADS):
            hs = slice(h * N_KEYS, (h + 1) * N_KEYS)
            rank_s[hs, :] = rank_ref[hs, :].astype(BF16)
            e2_s[hs, :] = e2_ref[hs, :].astype(BF16)

    act_ref[...] = jnp.dot(u_ref[...], xt_ref[...], preferred_element_type=F32)
    zero = jnp.zeros((BF16_ROWS, tn), BF16)
    assert n_i == SUBLANES
    tiles = []
    for h in range(PEER_HEADS):
        base = pl.multiple_of(h * N_KEYS + c * n_i, SUBLANES)
        tiles.append((n_ref[pl.ds(base, SUBLANES), :], e1_ref[pl.ds(base, SUBLANES), :]))
    for ii in range(n_i):
        g = [zero] * (N_KEYS // BF16_ROWS)
        for h in range(PEER_HEADS):
            n_row = jnp.broadcast_to(tiles[h][0][ii:ii + 1, :], (BF16_ROWS, tn)).astype(BF16)
            e1_row = jnp.broadcast_to(tiles[h][1][ii:ii + 1, :], (BF16_ROWS, tn)).astype(BF16)
            for t in range(N_KEYS // BF16_ROWS):
                ks = slice(h * N_KEYS + t * BF16_ROWS, h * N_KEYS + (t + 1) * BF16_ROWS)
                g[t] = g[t] + jnp.where(rank_s[ks, :] < n_row, e2_s[ks, :], zero) * e1_row
        for t in range(N_KEYS // BF16_ROWS):
            rs = slice(ii * N_KEYS + t * BF16_ROWS, ii * N_KEYS + (t + 1) * BF16_ROWS)
            w_ref[rs, :] = g[t] * _gelu_sigmoid(act_ref[rs, :]).astype(BF16)
    acc_ref[...] += jnp.dot(vt_ref[...], w_ref[...], preferred_element_type=F32)

    @pl.when(c == pl.num_programs(1) - 1)
    def _():
        x2 = x1_ref[...] + acc_ref[...].T
        y_ref[...] = _rms(x2, gfin_ref[...])


def _peer(xt, u_bf, vt_bf, n, e1, rank, e2, x1, g_final):
    d, n_tok = xt.shape
    n_exp = u_bf.shape[0]
    tn = _tile(n_tok, 512)
    ec = EXPERT_CHUNK
    rows = PEER_HEADS * N_KEYS
    tok = lambda i, c: (0, i)
    sc = pl.BlockSpec((rows, tn), tok)
    return pl.pallas_call(
        _peer_kernel,
        grid=(n_tok // tn, n_exp // ec),
        in_specs=[pl.BlockSpec((d, tn), tok),
                  pl.BlockSpec((ec, d), lambda i, c: (c, 0)),
                  pl.BlockSpec((d, ec), lambda i, c: (0, c)),
                  sc, sc, sc, sc,
                  pl.BlockSpec((tn, d), lambda i, c: (i, 0)),
                  pl.BlockSpec((1, d), lambda i, c: (0, 0))],
        out_specs=pl.BlockSpec((tn, d), lambda i, c: (i, 0)),
        out_shape=jax.ShapeDtypeStruct((n_tok, d), F32),
        scratch_shapes=[pltpu.VMEM((d, tn), F32), pltpu.VMEM((ec, tn), BF16), pltpu.VMEM((ec, tn), F32),
                        pltpu.VMEM((rows, tn), BF16), pltpu.VMEM((rows, tn), BF16)],
        compiler_params=_params("parallel", "arbitrary"),
        name="peer_experts",
    )(xt, u_bf, vt_bf, n, e1, rank, e2, x1, g_final)


def _block_diag(w):
    return jax.scipy.linalg.block_diag(*[w[i] for i in range(w.shape[0])])


def kernel(x_prompt, x_sample, cache_k, cache_v, state_conv, state_h, g_mix, w_in, conv_w, conv_b, w_a, b_a, w_x,
           b_x, lru_lambda, g_rnn_out, g_attn_out, w_out, g_ffn, w_query, sub_keys, expert_u, expert_v, g_final):
    depth = g_mix.shape[0]
    assert depth == 1
    bp, sp, d = x_prompt.shape
    bs, ts, _ = x_sample.shape
    assert ts >= CONV_WIDTH - 1
    l = 0
    w_in_bf = w_in[l].astype(BF16)
    w_out_bf = w_out[l].astype(BF16)
    wa = _block_diag(w_a[l]).astype(BF16)
    wx = _block_diag(w_x[l]).astype(BF16)
    row = lambda v: v.reshape(1, -1)
    cw, cb, ba, bx, lam = conv_w[l], row(conv_b[l]), row(b_a[l]), row(b_x[l]), row(lru_lambda[l])
    wq_t = w_query[l].T.astype(BF16)
    keys = sub_keys[l].reshape(2 * PEER_HEADS, N_KEYS, PEER_HALF).astype(BF16)
    u_bf = expert_u[l].astype(BF16)
    vt_bf = expert_v[l].T.astype(BF16)
    gfin = row(g_final)

    def ffn(y_rnn, y_attn, x):
        x1, xt = _merge(y_rnn, y_attn, x, row(g_rnn_out[l]), row(g_attn_out[l]), w_out_bf, row(g_ffn[l]))
        n, e1, rank, e2 = _route(xt, wq_t, keys)
        return _peer(xt, u_bf, vt_bf, n, e1, rank, e2, x1, gfin)

    xp = x_prompt.reshape(bp * sp, d)
    xr, gate, q, k, v, kf, vf = _inproj(xp, g_mix[l], w_in_bf)
    seq = lambda a: a.reshape(bp, sp, -1)
    y_rnn, h_p = _rglru_prompt(seq(xr), seq(gate), cw, cb, wa, ba, wx, bx, lam)
    y_attn = _attn_prompt(seq(q), seq(k), seq(v))
    y_p = ffn(y_rnn.reshape(bp * sp, -1), y_attn.reshape(bp * sp, -1), xp).reshape(bp, sp, d)
    keep = min(MAX_WINDOW, sp)
    k_p = seq(kf)[:, sp - keep:].reshape(1, bp, keep, N_HEADS, HEAD_DIM)
    v_p = seq(vf)[:, sp - keep:].reshape(1, bp, keep, N_HEADS, HEAD_DIM)
    conv_p = seq(xr)[:, sp - (CONV_WIDTH - 1):][None]
    h_p = h_p.reshape(1, bp, D_RNN)

    xs = x_sample.reshape(bs * ts, d)
    xr, gate, q, k, v, kf, vf = _inproj(xs, g_mix[l], w_in_bf)
    seq = lambda a: a.reshape(bs, ts, -1)
    tmaj = lambda a: jnp.swapaxes(seq(a), 0, 1)
    y_t, h_s = _rglru_sample(tmaj(xr), tmaj(gate), jnp.swapaxes(state_conv[l], 0, 1), state_h[l],
                             cw, cb, wa, ba, wx, bx, lam)
    y_rnn = jnp.swapaxes(y_t, 0, 1)
    y_attn = _attn_sample(seq(q), seq(k), seq(v), cache_k[l], cache_v[l])
    y_s = ffn(y_rnn.reshape(bs * ts, -1), y_attn.reshape(bs * ts, -1), xs).reshape(bs, ts, d)
    k_s = seq(kf).reshape(1, bs, ts, N_HEADS, HEAD_DIM)
    v_s = seq(vf).reshape(1, bs, ts, N_HEADS, HEAD_DIM)
    conv_s = seq(xr)[:, ts - (CONV_WIDTH - 1):][None]
    h_s = h_s.reshape(1, bs, D_RNN)

    return (y_p, y_s, k_p, v_p, k_s, v_s, conv_p, conv_s, h_p, h_s)
```

```python
import functools

import numpy as np
import jax
import jax.numpy as jnp
from jax import lax
from jax.experimental import pallas as pl
from jax.experimental.pallas import tpu as pltpu

F32 = jnp.float32
BF16 = jnp.bfloat16

EPS = 1e-6
D_RNN = 512
D_ATTN = 512
N_HEADS = 8
HEAD_DIM = 64
CONV_WIDTH = 4
LRU_C = 8.0
DILATED_PATTERNS = ((128, 1), (512, 4), (2048, 16))
MAX_WINDOW = 2048
N_KEYS = 128
PEER_HEADS = 8
PEER_HALF = 128
PEER_TOPK = 16

LANES = 128
SUBLANES = 8
BF16_ROWS = 2 * SUBLANES
HEAD_PAIR = LANES
NEG = -1e30
VMEM_LIMIT = 48 * 1024 * 1024


def _params(*sem):
    return pltpu.CompilerParams(dimension_semantics=sem, vmem_limit_bytes=VMEM_LIMIT)


def _tile(n, pref):
    return pref if n % pref == 0 else n


def _rms(x, g):
    return x * lax.rsqrt(jnp.mean(x * x, axis=-1, keepdims=True) + EPS) * g


def _gelu(x):
    c = np.float32(np.sqrt(2.0 / np.pi))
    return x * (0.5 * (1.0 + jnp.tanh(c * (x + 0.044715 * (x * x * x)))))


def _sigmoid(x):
    return 1.0 / (1.0 + jnp.exp(-x))


def _multiplicity(delta):
    delta = np.asarray(delta)
    cnt = np.zeros(delta.shape, np.int32)
    for window, dil in DILATED_PATTERNS:
        cnt += ((delta >= 0) & (delta <= window) & (delta % dil == 0)).astype(np.int32)
    return cnt


def _log_multiplicity(delta, same=None):
    cnt = jnp.zeros(delta.shape, F32)
    for window, dil in DILATED_PATTERNS:
        cnt = cnt + ((delta >= 0) & (delta <= window) & (delta % dil == 0)).astype(F32)
    ok = cnt > 0
    if same is not None:
        ok = ok & same
    return jnp.where(ok, jnp.log(jnp.maximum(cnt, 1.0)), NEG)


def _inproj_kernel(x_ref, g_ref, w_ref, xr_ref, gate_ref, q_ref, k_ref, v_ref, kf_ref, vf_ref):
    xn = _rms(x_ref[...], g_ref[...])
    z = jnp.dot(xn.astype(BF16), w_ref[...], preferred_element_type=F32)
    xr_ref[...] = z[:, 0:D_RNN]
    gate_ref[...] = z[:, D_RNN:2 * D_RNN]
    o = 2 * D_RNN
    q_ref[...] = (z[:, o:o + D_ATTN] * (HEAD_DIM ** -0.5)).astype(BF16)
    k = z[:, o + D_ATTN:o + 2 * D_ATTN]
    v = z[:, o + 2 * D_ATTN:o + 3 * D_ATTN]
    k_ref[...] = k.astype(BF16)
    v_ref[...] = v.astype(BF16)
    kf_ref[...] = k
    vf_ref[...] = v


def _inproj(x, g_mix, w_in_bf):
    n, d = x.shape
    d_in = w_in_bf.shape[1]
    tm = _tile(n, 512)
    row = lambda i: (i, 0)
    full = lambda i: (0, 0)
    o512 = pl.BlockSpec((tm, 512), row)
    return pl.pallas_call(
        _inproj_kernel,
        grid=(n // tm,),
        in_specs=[pl.BlockSpec((tm, d), row), pl.BlockSpec((1, d), full), pl.BlockSpec((d, d_in), full)],
        out_specs=[o512] * 7,
        out_shape=[jax.ShapeDtypeStruct((n, 512), F32), jax.ShapeDtypeStruct((n, 512), F32),
                   jax.ShapeDtypeStruct((n, 512), BF16), jax.ShapeDtypeStruct((n, 512), BF16),
                   jax.ShapeDtypeStruct((n, 512), BF16), jax.ShapeDtypeStruct((n, 512), F32),
                   jax.ShapeDtypeStruct((n, 512), F32)],
        compiler_params=_params("parallel"),
        name="inproj",
    )(x, g_mix.reshape(1, d), w_in_bf)


def _lru_gates(xc, wa, ba, wx, bx, lam):
    xb = xc.astype(BF16)
    r = _sigmoid(jnp.dot(xb, wa, preferred_element_type=F32) + ba)
    i = _sigmoid(jnp.dot(xb, wx, preferred_element_type=F32) + bx)
    nl = -lam
    softplus = jnp.maximum(nl, 0.0) + jnp.log1p(jnp.exp(-jnp.abs(nl)))
    log_a = (-LRU_C) * r * softplus
    a = jnp.exp(log_a)
    u = jnp.sqrt(-jnp.tanh(log_a) * (a * a + 1.0)) * (i * xc)
    return a, u


def _rglru_prompt_kernel(xr_ref, gate_ref, cw_ref, cb_ref, wa_ref, ba_ref, wx_ref, bx_ref, lam_ref,
                         y_ref, hlast_ref, ext_ref, a_ref, u_ref, hc_ref):
    t = pl.program_id(1)
    tt = xr_ref.shape[1]

    @pl.when(t == 0)
    def _():
        ext_ref[0:SUBLANES, :] = jnp.zeros((SUBLANES, D_RNN), F32)
        hc_ref[...] = jnp.zeros(hc_ref.shape, F32)

    ext_ref[SUBLANES:SUBLANES + tt, :] = xr_ref[0]
    cw = cw_ref[...]
    xc = cb_ref[...]
    for w in range(CONV_WIDTH):
        lo = SUBLANES - (CONV_WIDTH - 1) + w
        xc = xc + cw[w:w + 1, :] * ext_ref[lo:lo + tt, :]
    ext_ref[0:SUBLANES, :] = ext_ref[tt:tt + SUBLANES, :]

    a, u = _lru_gates(xc, wa_ref[...], ba_ref[...], wx_ref[...], bx_ref[...], lam_ref[...])
    a_ref[...] = a
    u_ref[...] = u

    row = lax.broadcasted_iota(jnp.int32, (SUBLANES, D_RNN), 0)

    def group(g, hc):
        off = pl.multiple_of(g * SUBLANES, SUBLANES)
        a8 = a_ref[pl.ds(off, SUBLANES), :]
        u8 = u_ref[pl.ds(off, SUBLANES), :]
        for s in (1, 2, 4):
            a_sh = pltpu.roll(a8, s, 0)
            u_sh = pltpu.roll(u8, s, 0)
            m = row >= s
            u8 = jnp.where(m, u8 + a8 * u_sh, u8)
            a8 = jnp.where(m, a8 * a_sh, a8)
        h8 = u8 + a8 * hc
        u_ref[pl.ds(off, SUBLANES), :] = h8
        return h8[SUBLANES - 1:SUBLANES, :]

    hc = lax.fori_loop(0, tt // SUBLANES, group, hc_ref[0:1, :])
    hc_ref[...] = jnp.broadcast_to(hc, hc_ref.shape)
    hlast_ref[0] = hc
    y_ref[0] = u_ref[...] * _gelu(gate_ref[0])


def _rglru_prompt(xr, gate, cw, cb, wa, ba, wx, bx, lam):
    b, t, c = xr.shape
    tt = _tile(t, 512)
    seq = lambda i, j: (i, j, 0)
    vec = lambda i, j: (0, 0)
    return pl.pallas_call(
        _rglru_prompt_kernel,
        grid=(b, t // tt),
        in_specs=[pl.BlockSpec((1, tt, c), seq), pl.BlockSpec((1, tt, c), seq),
                  pl.BlockSpec((CONV_WIDTH, c), vec), pl.BlockSpec((1, c), vec),
                  pl.BlockSpec((c, c), vec), pl.BlockSpec((1, c), vec),
                  pl.BlockSpec((c, c), vec), pl.BlockSpec((1, c), vec), pl.BlockSpec((1, c), vec)],
        out_specs=[pl.BlockSpec((1, tt, c), seq), pl.BlockSpec((1, 1, c), lambda i, j: (i, 0, 0))],
        out_shape=[jax.ShapeDtypeStruct((b, t, c), F32), jax.ShapeDtypeStruct((b, 1, c), F32)],
        scratch_shapes=[pltpu.VMEM((tt + SUBLANES, c), F32), pltpu.VMEM((tt, c), F32),
                        pltpu.VMEM((tt, c), F32), pltpu.VMEM((SUBLANES, c), F32)],
        compiler_params=_params("parallel", "arbitrary"),
        name="rglru_prompt",
    )(xr, gate, cw, cb, wa, ba, wx, bx, lam)


def _rglru_sample_kernel(xr_ref, gate_ref, cp_ref, h0_ref, cw_ref, cb_ref, wa_ref, ba_ref, wx_ref, bx_ref,
                         lam_ref, y_ref, hlast_ref):
    t_steps = xr_ref.shape[0]
    rows = [cp_ref[i] for i in range(CONV_WIDTH - 1)] + [xr_ref[i] for i in range(t_steps)]
    cw = cw_ref[...]
    h = h0_ref[...]
    for t in range(t_steps):
        xc = cb_ref[...]
        for w in range(CONV_WIDTH):
            xc = xc + cw[w:w + 1, :] * rows[t + w]
        a, u = _lru_gates(xc, wa_ref[...], ba_ref[...], wx_ref[...], bx_ref[...], lam_ref[...])
        h = a * h + u
        y_ref[t] = h * _gelu(gate_ref[t])
    hlast_ref[...] = h


def _rglru_sample(xr_t, gate_t, cp_t, h0, cw, cb, wa, ba, wx, bx, lam):
    t, b, c = xr_t.shape
    return pl.pallas_call(
        _rglru_sample_kernel,
        out_shape=[jax.ShapeDtypeStruct((t, b, c), F32), jax.ShapeDtypeStruct((b, c), F32)],
        compiler_params=pltpu.CompilerParams(vmem_limit_bytes=VMEM_LIMIT),
        name="rglru_sample",
    )(xr_t, gate_t, cp_t, h0, cw, cb, wa, ba, wx, bx, lam)


ATTN_TQ = 256
ATTN_WINDOWS = 3
NT_DIMS = (((1,), (1,)), ((), ()))


def _attn_prompt_kernel(lens, q_ref, k_ref, v_ref, bias_ref, o_ref):
    i = pl.program_id(2)
    tq = q_ref.shape[1]
    w = bias_ref.shape[2]
    last = (i + 1) * tq
    start = pl.multiple_of(jnp.maximum(last - w, 0), LANES)
    q = q_ref[0]
    even = lax.broadcasted_iota(jnp.int32, (tq, HEAD_PAIR), 1) < HEAD_DIM
    zero = jnp.zeros_like(q)

    def run(wlen):
        kw = k_ref[0, pl.ds(start, wlen), :]
        vw = v_ref[0, pl.ds(start, wlen), :]
        bias = bias_ref[0, :, 0:wlen]

        def head(qh):
            s = lax.dot_general(qh, kw, NT_DIMS, preferred_element_type=F32) + bias
            m = jnp.max(s, axis=-1, keepdims=True)
            p = jnp.exp(s - m)
            den = jnp.sum(p, axis=-1, keepdims=True)
            return jnp.dot(p.astype(BF16), vw, preferred_element_type=F32) / den

        o_ref[0] = jnp.where(even, head(jnp.where(even, q, zero)), head(jnp.where(even, zero, q)))

    for v, wlen in enumerate(lens):
        lo = lens[v - 1] if v > 0 else 0
        fits = last > lo
        if v + 1 < len(lens):
            fits = fits & (last <= wlen)
        pl.when(fits)(functools.partial(run, wlen))


def _attn_prompt(q, k, v):
    b, s, _ = q.shape
    tq = _tile(s, ATTN_TQ)
    w = min(s, MAX_WINDOW + tq)
    assert w % tq == 0 and w % LANES == 0
    n_off = w // tq
    ti = lax.broadcasted_iota(jnp.int32, (n_off, tq, w), 0)
    r = lax.broadcasted_iota(jnp.int32, (n_off, tq, w), 1)
    c = lax.broadcasted_iota(jnp.int32, (n_off, tq, w), 2)
    bias = _log_multiplicity(ti * tq + r - c)
    lens = tuple(sorted({-(-n_off * (v + 1) // ATTN_WINDOWS) * tq for v in range(ATTN_WINDOWS)}))
    return pl.pallas_call(
        functools.partial(_attn_prompt_kernel, lens),
        grid=(b, D_ATTN // HEAD_PAIR, s // tq),
        in_specs=[pl.BlockSpec((1, tq, HEAD_PAIR), lambda bi, hp, i: (bi, i, hp)),
                  pl.BlockSpec((1, s, HEAD_PAIR), lambda bi, hp, i: (bi, 0, hp)),
                  pl.BlockSpec((1, s, HEAD_PAIR), lambda bi, hp, i: (bi, 0, hp)),
                  pl.BlockSpec((1, tq, w), lambda bi, hp, i: (jnp.minimum(i, n_off - 1), 0, 0))],
        out_specs=pl.BlockSpec((1, tq, HEAD_PAIR), lambda bi, hp, i: (bi, i, hp)),
        out_shape=jax.ShapeDtypeStruct((b, s, D_ATTN), F32),
        compiler_params=_params("parallel", "parallel", "arbitrary"),
        name="attn_prompt",
    )(q, k, v, bias)


SAMPLE_ROWS = 16


def _attn_sample_kernel(t_new, q_ref, kt_ref, vt_ref, kn_ref, vn_ref, bias_ref, o_ref):
    bias = bias_ref[...]
    row = lax.broadcasted_iota(jnp.int32, (SAMPLE_ROWS, 1), 0)
    for h in range(N_HEADS):
        q = q_ref[0, h]
        s = jnp.dot(q, kt_ref[0, h].astype(BF16), preferred_element_type=F32) + bias
        qf = q.astype(F32)
        kn = kn_ref[0, h].astype(F32)
        vn = vn_ref[0, h].astype(F32)
        ln = []
        for jj in range(t_new):
            l = jnp.sum(qf * kn[jj:jj + 1, :], axis=-1, keepdims=True)
            for delta in range(t_new - jj):
                lb = float(np.log(max(int(_multiplicity(delta)), 1)))
                if lb != 0.0:
                    l = l + jnp.where(row - jj == delta, np.float32(lb), np.float32(0.0))
            ln.append(jnp.where(row >= jj, l, NEG))
        m = functools.reduce(jnp.maximum, ln, jnp.max(s, axis=-1, keepdims=True))
        p = jnp.exp(s - m)
        den = jnp.sum(p, axis=-1, keepdims=True)
        acc = lax.dot_general(p.astype(BF16), vt_ref[0, h].astype(BF16), NT_DIMS, preferred_element_type=F32)
        for jj in range(t_new):
            pn = jnp.exp(ln[jj] - m)
            den = den + pn
            acc = acc + pn * vn[jj:jj + 1, :]
        o_ref[0, h] = acc / den


def _attn_sample(q, k_new, v_new, cache_k, cache_v):
    b, t, _ = q.shape
    p = cache_k.shape[1]
    assert t <= SAMPLE_ROWS and all(_multiplicity(d) > 0 for d in range(t))
    heads = lambda a: jnp.pad(a.reshape(b, t, N_HEADS, HEAD_DIM).transpose(0, 2, 1, 3),
                              ((0, 0), (0, 0), (0, SAMPLE_ROWS - t), (0, 0)))
    lanes = lambda a: a.transpose(0, 2, 3, 1)
    jq = lax.broadcasted_iota(jnp.int32, (SAMPLE_ROWS, p), 0)
    kp = lax.broadcasted_iota(jnp.int32, (SAMPLE_ROWS, p), 1)
    bias = jnp.where(jq < t, _log_multiplicity(p + jq - kp), 0.0)
    new = pl.BlockSpec((1, N_HEADS, SAMPLE_ROWS, HEAD_DIM), lambda i: (i, 0, 0, 0))
    old = pl.BlockSpec((1, N_HEADS, HEAD_DIM, p), lambda i: (i, 0, 0, 0))
    out = pl.pallas_call(
        functools.partial(_attn_sample_kernel, t),
        grid=(b,),
        in_specs=[new, old, old, new, new, pl.BlockSpec((SAMPLE_ROWS, p), lambda i: (0, 0))],
        out_specs=new,
        out_shape=jax.ShapeDtypeStruct((b, N_HEADS, SAMPLE_ROWS, HEAD_DIM), F32),
        compiler_params=_params("parallel"),
        name="attn_sample",
    )(heads(q), lanes(cache_k), lanes(cache_v), heads(k_new), heads(v_new), bias)
    return out[:, :, :t].transpose(0, 2, 1, 3).reshape(b, t, D_ATTN)


def _merge_kernel(yr_ref, ya_ref, x_ref, gr_ref, ga_ref, wo_ref, gf_ref, x1_ref, xt_ref):
    a = _rms(yr_ref[...], gr_ref[...]).astype(BF16)
    b = _rms(ya_ref[...], ga_ref[...]).astype(BF16)
    merged = (jnp.dot(a, wo_ref[0:D_RNN, :], preferred_element_type=F32)
              + jnp.dot(b, wo_ref[D_RNN:D_RNN + D_ATTN, :], preferred_element_type=F32))
    x1 = x_ref[...] + merged
    x1_ref[...] = x1
    xt_ref[...] = _rms(x1, gf_ref[...]).T.astype(BF16)


def _merge(y_rnn, y_attn, x, g_rnn, g_attn, w_out_bf, g_ffn):
    n, d = x.shape
    tm = _tile(n, 512)
    row = lambda i: (i, 0)
    full = lambda i: (0, 0)
    return pl.pallas_call(
        _merge_kernel,
        grid=(n // tm,),
        in_specs=[pl.BlockSpec((tm, D_RNN), row), pl.BlockSpec((tm, D_ATTN), row), pl.BlockSpec((tm, d), row),
                  pl.BlockSpec((1, D_RNN), full), pl.BlockSpec((1, D_ATTN), full),
                  pl.BlockSpec((D_RNN + D_ATTN, d), full), pl.BlockSpec((1, d), full)],
        out_specs=[pl.BlockSpec((tm, d), row), pl.BlockSpec((d, tm), lambda i: (0, i))],
        out_shape=[jax.ShapeDtypeStruct((n, d), F32), jax.ShapeDtypeStruct((d, n), BF16)],
        compiler_params=_params("parallel"),
        name="merge",
    )(y_rnn, y_attn, x, g_rnn, g_attn, w_out_bf, g_ffn)


N_SORT = PEER_TOPK + 1
ROUTE_LANES = 4 * LANES


def _top_sorted(s, with_rank):
    tn = s.shape[1]
    row = lax.broadcasted_iota(jnp.int32, (SUBLANES, tn), 0)
    blocks = [jnp.full((SUBLANES, tn), NEG, F32) for _ in range(-(-N_SORT // SUBLANES))]
    rank = jnp.full(s.shape, float(PEER_TOPK), F32) if with_rank else None
    cur = s
    for k in range(N_SORT):
        mk = jnp.max(cur, axis=0, keepdims=True)
        blocks[k // SUBLANES] = jnp.where(row == k % SUBLANES, mk, blocks[k // SUBLANES])
        if k + 1 < N_SORT:
            hit = cur == mk
            if with_rank:
                rank = jnp.where(hit, float(k), rank)
            cur = jnp.where(hit, NEG, cur)
    return blocks, rank


def _candidates(a_blk, b_blk):
    a0, a1, a2 = a_blk
    b0, b1, b2 = b_blk
    tn = a0.shape[1]
    row = lax.broadcasted_iota(jnp.int32, (SUBLANES, tn), 0)
    out = [a0 + b0[0:1, :], a1 + b0[0:1, :], b1 + a0[0:1, :]]
    for l in range(1, SUBLANES):
        n_k = N_SORT // (l + 1)
        c = a0 + b0[l:l + 1, :]
        out.append(c if n_k >= SUBLANES else jnp.where(row < n_k, c, NEG))
    out.append(jnp.where(row == 0, a2[0:1, :] + b0[0:1, :], jnp.where(row == 1, b2[0:1, :] + a0[0:1, :], NEG)))
    return out


def _route_kernel(xt_ref, wq_ref, keys_ref, n_ref, e1_ref, rank_ref, e2_ref, q_ref):
    q_ref[...] = jnp.dot(wq_ref[...], xt_ref[...], preferred_element_type=F32).astype(BF16)

    def head(h, ls):
        r1 = pl.multiple_of(h * (2 * PEER_HALF), 2 * PEER_HALF)
        r2 = pl.multiple_of(r1 + PEER_HALF, PEER_HALF)
        s1 = jnp.dot(keys_ref[2 * h], q_ref[pl.ds(r1, PEER_HALF), ls], preferred_element_type=F32)
        s2 = jnp.dot(keys_ref[2 * h + 1], q_ref[pl.ds(r2, PEER_HALF), ls], preferred_element_type=F32)
        a_blk, _ = _top_sorted(s1, False)
        b_blk, rank2 = _top_sorted(s2, True)
        cands = _candidates(a_blk, b_blk)
        cur = list(cands)
        kth = None
        for k in range(N_SORT):
            mx = functools.reduce(jnp.maximum, cur)
            mx = jnp.max(mx, axis=0, keepdims=True)
            if k == PEER_TOPK - 1:
                kth = mx
            if k + 1 < N_SORT:
                cur = [jnp.where(c == mx, NEG, c) for c in cur]
        tau = 0.5 * (kth + mx)
        m1 = a_blk[0][0:1, :]
        m2 = b_blk[0][0:1, :]
        z = functools.reduce(
            lambda x, y: x + y,
            [jnp.where(c >= tau, jnp.exp(c - (m1 + m2)), 0.0) for c in cands])
        z = jnp.sum(z, axis=0, keepdims=True)
        thr = tau - s1
        n = jnp.zeros(s1.shape, F32)
        for l in range(PEER_TOPK):
            b_l = b_blk[l // SUBLANES][l % SUBLANES:l % SUBLANES + 1, :]
            n = n + jnp.where(b_l >= thr, 1.0, 0.0)
        o = pl.multiple_of(h * N_KEYS, N_KEYS)
        n_ref[pl.ds(o, N_KEYS), ls] = n
        e1_ref[pl.ds(o, N_KEYS), ls] = jnp.exp(s1 - m1) / z
        rank_ref[pl.ds(o, N_KEYS), ls] = rank2
        e2_ref[pl.ds(o, N_KEYS), ls] = jnp.exp(s2 - m2)

    tn = xt_ref.shape[1]
    lb = min(tn, ROUTE_LANES)
    for b in range(tn // lb):
        ls = slice(b * lb, (b + 1) * lb)

        def body(h, carry, ls=ls):
            head(h, ls)
            return carry

        lax.fori_loop(0, PEER_HEADS, body, 0)


def _route(xt, wq_t, keys):
    d, n = xt.shape
    tn = _tile(n, 512)
    rows = PEER_HEADS * N_KEYS
    col = lambda i: (0, i)
    o = pl.BlockSpec((rows, tn), col)
    return pl.pallas_call(
        _route_kernel,
        grid=(n // tn,),
        in_specs=[pl.BlockSpec((d, tn), col), pl.BlockSpec(wq_t.shape, lambda i: (0, 0)),
                  pl.BlockSpec(keys.shape, lambda i: (0, 0, 0))],
        out_specs=[o] * 4,
        out_shape=[jax.ShapeDtypeStruct((rows, n), F32)] * 4,
        scratch_shapes=[pltpu.VMEM((wq_t.shape[0], tn), BF16)],
        compiler_params=_params("parallel"),
        name="peer_route",
    )(xt, wq_t, keys)


EXPERT_CHUNK = 2048
PEER_VMEM_LIMIT = 58 * 1024 * 1024


def _gelu_sigmoid(x):
    c = float(np.sqrt(2.0 / np.pi))
    z2 = x * ((-2.0 * c) + (-2.0 * c * 0.044715) * (x * x))
    return x / (1.0 + jnp.exp(z2))


def _peer_kernel(xt_ref, u_ref, vt_ref, n_ref, e1_ref, rank_ref, e2_ref, x1_ref, gfin_ref, y_ref,
                 acc_ref, w_ref, act_ref, rank_s, e2_s):
    c = pl.program_id(1)
    ec = u_ref.shape[0]
    tn = xt_ref.shape[1]
    n_i = ec // N_KEYS

    @pl.when(c == 0)
    def _():
        acc_ref[...] = jnp.zeros(acc_ref.shape, F32)
        for h in range(PEER_HEADS):
            hs = slice(h * N_KEYS, (h + 1) * N_KEYS)
            rank_s[hs, :] = rank_ref[hs, :].astype(BF16)
            e2_s[hs, :] = e2_ref[hs, :].astype(BF16)

    act_ref[...] = jnp.dot(u_ref[...], xt_ref[...], preferred_element_type=F32)
    zero = jnp.zeros((BF16_ROWS, tn), BF16)
    assert n_i % SUBLANES == 0
    tiles = {}
    for h in range(PEER_HEADS):
        for q8 in range(n_i // SUBLANES):
            base = pl.multiple_of(h * N_KEYS + c * n_i + q8 * SUBLANES, SUBLANES)
            tiles[h, q8] = (n_ref[pl.ds(base, SUBLANES), :], e1_ref[pl.ds(base, SUBLANES), :])
    for ii in range(n_i):
        g = [zero] * (N_KEYS // BF16_ROWS)
        for h in range(PEER_HEADS):
            n_tile, e1_tile = tiles[h, ii // SUBLANES]
            i8 = ii % SUBLANES
            n_row = jnp.broadcast_to(n_tile[i8:i8 + 1, :], (BF16_ROWS, tn)).astype(BF16)
            e1_row = jnp.broadcast_to(e1_tile[i8:i8 + 1, :], (BF16_ROWS, tn)).astype(BF16)
            for t in range(N_KEYS // BF16_ROWS):
                ks = slice(h * N_KEYS + t * BF16_ROWS, h * N_KEYS + (t + 1) * BF16_ROWS)
                g[t] = g[t] + jnp.where(rank_s[ks, :] < n_row, e2_s[ks, :], zero) * e1_row
        for t in range(N_KEYS // BF16_ROWS):
            rs = slice(ii * N_KEYS + t * BF16_ROWS, ii * N_KEYS + (t + 1) * BF16_ROWS)
            w_ref[rs, :] = g[t] * _gelu_sigmoid(act_ref[rs, :]).astype(BF16)
    acc_ref[...] += jnp.dot(vt_ref[...], w_ref[...], preferred_element_type=F32)

    @pl.when(c == pl.num_programs(1) - 1)
    def _():
        x2 = x1_ref[...] + acc_ref[...].T
        y_ref[...] = _rms(x2, gfin_ref[...])


def _peer(xt, u_bf, vt_bf, n, e1, rank, e2, x1, g_final):
    d, n_tok = xt.shape
    n_exp = u_bf.shape[0]
    tn = _tile(n_tok, 512)
    ec = EXPERT_CHUNK
    rows = PEER_HEADS * N_KEYS
    tok = lambda i, c: (0, i)
    sc = pl.BlockSpec((rows, tn), tok)
    return pl.pallas_call(
        _peer_kernel,
        grid=(n_tok // tn, n_exp // ec),
        in_specs=[pl.BlockSpec((d, tn), tok),
                  pl.BlockSpec((ec, d), lambda i, c: (c, 0)),
                  pl.BlockSpec((d, ec), lambda i, c: (0, c)),
                  sc, sc, sc, sc,
                  pl.BlockSpec((tn, d), lambda i, c: (i, 0)),
                  pl.BlockSpec((1, d), lambda i, c: (0, 0))],
        out_specs=pl.BlockSpec((tn, d), lambda i, c: (i, 0)),
        out_shape=jax.ShapeDtypeStruct((n_tok, d), F32),
        scratch_shapes=[pltpu.VMEM((d, tn), F32), pltpu.VMEM((ec, tn), BF16), pltpu.VMEM((ec, tn), F32),
                        pltpu.VMEM((rows, tn), BF16), pltpu.VMEM((rows, tn), BF16)],
        compiler_params=pltpu.CompilerParams(dimension_semantics=("parallel", "arbitrary"),
                                             vmem_limit_bytes=PEER_VMEM_LIMIT),
        name="peer_experts",
    )(xt, u_bf, vt_bf, n, e1, rank, e2, x1, g_final)


def _block_diag(w):
    return jax.scipy.linalg.block_diag(*[w[i] for i in range(w.shape[0])])


def kernel(x_prompt, x_sample, cache_k, cache_v, state_conv, state_h, g_mix, w_in, conv_w, conv_b, w_a, b_a, w_x,
           b_x, lru_lambda, g_rnn_out, g_attn_out, w_out, g_ffn, w_query, sub_keys, expert_u, expert_v, g_final):
    depth = g_mix.shape[0]
    assert depth == 1
    bp, sp, d = x_prompt.shape
    bs, ts, _ = x_sample.shape
    assert ts >= CONV_WIDTH - 1
    l = 0
    w_in_bf = w_in[l].astype(BF16)
    w_out_bf = w_out[l].astype(BF16)
    wa = _block_diag(w_a[l]).astype(BF16)
    wx = _block_diag(w_x[l]).astype(BF16)
    row = lambda v: v.reshape(1, -1)
    cw, cb, ba, bx, lam = conv_w[l], row(conv_b[l]), row(b_a[l]), row(b_x[l]), row(lru_lambda[l])
    wq_t = w_query[l].T.astype(BF16)
    keys = sub_keys[l].reshape(2 * PEER_HEADS, N_KEYS, PEER_HALF).astype(BF16)
    u_bf = expert_u[l].astype(BF16)
    vt_bf = expert_v[l].T.astype(BF16)
    gfin = row(g_final)

    def ffn(y_rnn, y_attn, x):
        x1, xt = _merge(y_rnn, y_attn, x, row(g_rnn_out[l]), row(g_attn_out[l]), w_out_bf, row(g_ffn[l]))
        n, e1, rank, e2 = _route(xt, wq_t, keys)
        return _peer(xt, u_bf, vt_bf, n, e1, rank, e2, x1, gfin)

    xp = x_prompt.reshape(bp * sp, d)
    xr, gate, q, k, v, kf, vf = _inproj(xp, g_mix[l], w_in_bf)
    seq = lambda a: a.reshape(bp, sp, -1)
    y_rnn, h_p = _rglru_prompt(seq(xr), seq(gate), cw, cb, wa, ba, wx, bx, lam)
    y_attn = _attn_prompt(seq(q), seq(k), seq(v))
    y_p = ffn(y_rnn.reshape(bp * sp, -1), y_attn.reshape(bp * sp, -1), xp).reshape(bp, sp, d)
    keep = min(MAX_WINDOW, sp)
    k_p = seq(kf)[:, sp - keep:].reshape(1, bp, keep, N_HEADS, HEAD_DIM)
    v_p = seq(vf)[:, sp - keep:].reshape(1, bp, keep, N_HEADS, HEAD_DIM)
    conv_p = seq(xr)[:, sp - (CONV_WIDTH - 1):][None]
    h_p = h_p.reshape(1, bp, D_RNN)

    xs = x_sample.reshape(bs * ts, d)
    xr, gate, q, k, v, kf, vf = _inproj(xs, g_mix[l], w_in_bf)
    seq = lambda a: a.reshape(bs, ts, -1)
    tmaj = lambda a: jnp.swapaxes(seq(a), 0, 1)
    y_t, h_s = _rglru_sample(tmaj(xr), tmaj(gate), jnp.swapaxes(state_conv[l], 0, 1), state_h[l],
                             cw, cb, wa, ba, wx, bx, lam)
    y_rnn = jnp.swapaxes(y_t, 0, 1)
    y_attn = _attn_sample(seq(q), seq(k), seq(v), cache_k[l], cache_v[l])
    y_s = ffn(y_rnn.reshape(bs * ts, -1), y_attn.reshape(bs * ts, -1), xs).reshape(bs, ts, d)
    k_s = seq(kf).reshape(1, bs, ts, N_HEADS, HEAD_DIM)
    v_s = seq(vf).reshape(1, bs, ts, N_HEADS, HEAD_DIM)
    conv_s = seq(xr)[:, ts - (CONV_WIDTH - 1):][None]
    h_s = h_s.reshape(1, bs, D_RNN)

    return (y_p, y_s, k_p, v_p, k_s, v_s, conv_p, conv_s, h_p, h_s)
```
